```python
import math
import jax, jax.numpy as jnp
from jax import lax
import numpy as np

D_MODEL = 4096
BATCH = 4
SEQ = 2048
DEPTH = 4

D_MIX = D_MODEL
ATT_WIDTH = D_MIX // 2
CONV_WIDTH = D_MIX - ATT_WIDTH

N_HEADS = 16
V_HEAD_DIM = ATT_WIDTH // N_HEADS
QK_NOPE_DIM = 128
QK_ROPE_DIM = 64
QK_HEAD_DIM = QK_NOPE_DIM + QK_ROPE_DIM
Q_LORA_RANK = 1024
KV_LORA_RANK = 512
ROPE_THETA = 10000.0
Q_BLOCK = 128

CONV_GROUPS = 16
CONV_KERNEL = 31

EPS = 1e-6

IN_SIZES = (Q_LORA_RANK, KV_LORA_RANK, QK_ROPE_DIM, ATT_WIDTH, 2 * CONV_WIDTH, CONV_WIDTH)
IN_COLS = sum(IN_SIZES)
IN_SPLITS = tuple(int(s) for s in np.cumsum(IN_SIZES)[:-1])

kernel_name = "hymba_mla_conformer_hybrid"


def rmsnorm(x, g):
    xf = x.astype(jnp.float32)
    y = xf * lax.rsqrt(jnp.mean(xf * xf, axis=-1, keepdims=True) + EPS)
    return (y * g.astype(jnp.float32)).astype(x.dtype)


def layernorm(x, g, b):
    xf = x.astype(jnp.float32)
    mu = jnp.mean(xf, axis=-1, keepdims=True)
    xc = xf - mu
    var = jnp.mean(xc * xc, axis=-1, keepdims=True)
    y = xc * lax.rsqrt(var + EPS)
    return (y * g.astype(jnp.float32) + b.astype(jnp.float32)).astype(x.dtype)


def rope_tables(positions, dtype):
    half = QK_ROPE_DIM // 2
    inv_freq = ROPE_THETA ** (-jnp.arange(half, dtype=jnp.float32) / half)
    ang = positions.astype(jnp.float32)[..., None] * inv_freq
    return jnp.cos(ang)[:, :, None, :].astype(dtype), jnp.sin(ang)[:, :, None, :].astype(dtype)


def apply_rope(x, cos, sin):
    x1, x2 = jnp.split(x, 2, axis=-1)
    return jnp.concatenate([x1 * cos - x2 * sin, x2 * cos + x1 * sin], axis=-1)


def causal_attention(q, k, v):
    B, S, H, Dq = q.shape
    Dv = v.shape[-1]
    nb = S // Q_BLOCK
    scale = 1.0 / math.sqrt(Dq)
    qb = q.reshape(B, nb, Q_BLOCK, H, Dq).transpose(1, 0, 2, 3, 4)
    kpos = jnp.arange(S)

    def one_block(args):
        i, qi = args
        s = jnp.einsum('bqhd,bkhd->bhqk', qi, k, preferred_element_type=jnp.float32) * scale
        qpos = i * Q_BLOCK + jnp.arange(Q_BLOCK)
        mask = kpos[None, :] <= qpos[:, None]
        s = jnp.where(mask[None, None], s, -jnp.inf)
        p = jax.nn.softmax(s, axis=-1)
        return jnp.einsum('bhqk,bkhd->bqhd', p.astype(v.dtype), v)

    out = lax.map(one_block, (jnp.arange(nb), qb))
    return out.transpose(1, 0, 2, 3, 4).reshape(B, S, H, Dv)


def causal_depthwise_conv(u, w, b):
    C = u.shape[-1]
    y = lax.conv_general_dilated(
        u, w[:, None, :].astype(u.dtype),
        window_strides=(1,), padding=((CONV_KERNEL - 1, 0),),
        dimension_numbers=('NWC', 'WIO', 'NWC'), feature_group_count=C)
    return y + b


def setup_inputs(seed: int = 0) -> dict:
    key = jax.random.key(seed)
    ks = jax.random.split(key, 16)
    f32 = jnp.float32

    def nrm(k, shape, scale):
        return jax.random.normal(k, shape, f32) * scale

    def gain(k, shape):
        return 1.0 + 0.01 * jax.random.normal(k, shape, f32)

    x = jax.random.normal(ks[0], (BATCH, SEQ, D_MODEL), f32)
    offset = jax.random.randint(ks[1], (BATCH, 1), 0, 4096, dtype=jnp.int32)
    positions = (jnp.arange(SEQ, dtype=jnp.int32)[None, :] + offset).astype(jnp.int32)
    return {
        "x": x,
        "positions": positions,
        "ln_g": gain(ks[2], (DEPTH, D_MODEL)),
        "w_in": nrm(ks[3], (DEPTH, D_MODEL, IN_COLS), D_MODEL ** -0.5),
        "q_a_norm": gain(ks[4], (DEPTH, Q_LORA_RANK)),
        "w_q_up": nrm(ks[5], (DEPTH, Q_LORA_RANK, N_HEADS * QK_HEAD_DIM), Q_LORA_RANK ** -0.5),
        "kv_a_norm": gain(ks[6], (DEPTH, KV_LORA_RANK)),
        "w_kv_up": nrm(ks[7], (DEPTH, KV_LORA_RANK, N_HEADS * (QK_NOPE_DIM + V_HEAD_DIM)), KV_LORA_RANK ** -0.5),
        "q_norm": gain(ks[8], (DEPTH, QK_HEAD_DIM)),
        "k_norm": gain(ks[9], (DEPTH, QK_HEAD_DIM)),
        "w_dw": nrm(ks[10], (DEPTH, CONV_KERNEL, CONV_WIDTH), CONV_KERNEL ** -0.5),
        "b_dw": nrm(ks[11], (DEPTH, CONV_WIDTH), 0.01),
        "conv_ln_g": gain(ks[12], (DEPTH, CONV_WIDTH)),
        "conv_ln_b": nrm(ks[13], (DEPTH, CONV_WIDTH), 0.01),
        "w_out": nrm(ks[14], (DEPTH, D_MIX, D_MODEL), D_MIX ** -0.5),
    }


def reference(x, positions, ln_g, w_in, q_a_norm, w_q_up, kv_a_norm, w_kv_up,
              q_norm, k_norm, w_dw, b_dw, conv_ln_g, conv_ln_b, w_out):
    B, S, _ = x.shape
    cos, sin = rope_tables(positions, x.dtype)
    for l in range(DEPTH):
        h = rmsnorm(x, ln_g[l])
        z = h @ w_in[l]
        q_c, kv_c, k_pe, g_att, u_conv, g_conv = jnp.split(z, IN_SPLITS, axis=-1)

        q = (rmsnorm(q_c, q_a_norm[l]) @ w_q_up[l]).reshape(B, S, N_HEADS, QK_HEAD_DIM)
        kv = (rmsnorm(kv_c, kv_a_norm[l]) @ w_kv_up[l]).reshape(B, S, N_HEADS, QK_NOPE_DIM + V_HEAD_DIM)
        k_nope, v = kv[..., :QK_NOPE_DIM], kv[..., QK_NOPE_DIM:]
        k_pe_h = jnp.broadcast_to(k_pe[:, :, None, :], (B, S, N_HEADS, QK_ROPE_DIM))
        k = jnp.concatenate([k_nope, k_pe_h], axis=-1)
        q = rmsnorm(q, q_norm[l])
        k = rmsnorm(k, k_norm[l])
        q = jnp.concatenate([q[..., :QK_NOPE_DIM], apply_rope(q[..., QK_NOPE_DIM:], cos, sin)], axis=-1)
        k = jnp.concatenate([k[..., :QK_NOPE_DIM], apply_rope(k[..., QK_NOPE_DIM:], cos, sin)], axis=-1)
        att = causal_attention(q, k, v).reshape(B, S, ATT_WIDTH)
        att = att * jax.nn.silu(g_att)

        a, b = jnp.split(u_conv, 2, axis=-1)
        u = a * jax.nn.sigmoid(b)
        c = causal_depthwise_conv(u, w_dw[l], b_dw[l])
        c = jax.nn.silu(layernorm(c, conv_ln_g[l], conv_ln_b[l]))
        c = c * jax.nn.silu(g_conv)

        y = jnp.concatenate([att, c], axis=-1) @ w_out[l]
        x = x + y
    return x
```

```python
import functools
import math

import jax
import jax.numpy as jnp
from jax import lax
from jax.experimental import pallas as pl
from jax.experimental.pallas import tpu as pltpu

F32 = jnp.float32
BF16 = jnp.bfloat16

N_HEADS = 16
NOPE = 128
ROPE = 64
HALF = ROPE // 2
QK_DIM = NOPE + ROPE
V_DIM = 128
Q_RANK = 1024
KV_RANK = 512
CONV_K = 31
ROPE_THETA = 10000.0
EPS = 1e-6

LANES = 128
BF16_SUBLANES = 16
VMEM_LIMIT = 56 * 1024 * 1024

HEADS_PER_STEP = 4
HALO = 32


def _cparams(sem):
    return pltpu.CompilerParams(dimension_semantics=sem, vmem_limit_bytes=VMEM_LIMIT)


def _norm_kernel(x_ref, g_ref, o_ref):
    x = x_ref[...]
    rinv = lax.rsqrt(jnp.mean(x * x, axis=-1, keepdims=True) + EPS)
    o_ref[...] = (x * rinv * g_ref[...]).astype(BF16)


def _norm(x, g, tm=256):
    T, D = x.shape
    return pl.pallas_call(
        _norm_kernel,
        grid=(T // tm,),
        in_specs=[pl.BlockSpec((tm, D), lambda i: (i, 0)),
                  pl.BlockSpec((1, D), lambda i: (0, 0))],
        out_specs=pl.BlockSpec((tm, D), lambda i: (i, 0)),
        out_shape=jax.ShapeDtypeStruct((T, D), BF16),
        compiler_params=_cparams(("parallel",)),
        name="norm",
    )(x, g)


def _matmul_kernel(a_ref, b_ref, o_ref):
    o_ref[...] = jnp.dot(a_ref[...], b_ref[...],
                         preferred_element_type=F32).astype(o_ref.dtype)


def _in_proj(h, w, tm=1024, tn=768):
    T, K = h.shape
    N = w.shape[1]
    return pl.pallas_call(
        _matmul_kernel,
        grid=(T // tm, N // tn),
        in_specs=[pl.BlockSpec((tm, K), lambda i, j: (i, 0)),
                  pl.BlockSpec((K, tn), lambda i, j: (0, j))],
        out_specs=pl.BlockSpec((tm, tn), lambda i, j: (i, j)),
        out_shape=jax.ShapeDtypeStruct((T, N), BF16),
        compiler_params=_cparams(("parallel", "arbitrary")),
        name="in_proj",
    )(h, w)


def _rope_pair(x, c4, s4):
    return x * c4 + pltpu.roll(x, 2 * HALF, axis=1) * s4


def _qkv_kernel(qc_ref, kvc_ref, kpe_ref, gqa_ref, gkva_ref, wq_ref, wk_ref, wvt_ref,
                gqn_ref, gq4_ref, gkn_ref, gk4_ref, c4_ref, s4_ref,
                q_ref, k_ref, vt_ref, qn_s, kvn_s):
    hg = pl.program_id(1)

    @pl.when(hg == 0)
    def _():
        qc = qc_ref[...].astype(F32)
        r = lax.rsqrt(jnp.mean(qc * qc, axis=-1, keepdims=True) + EPS)
        qn_s[...] = (qc * r * gqa_ref[...]).astype(BF16)
        kvc = kvc_ref[...].astype(F32)
        r = lax.rsqrt(jnp.mean(kvc * kvc, axis=-1, keepdims=True) + EPS)
        kvn_s[...] = (kvc * r * gkva_ref[...]).astype(BF16)

    qn = qn_s[...]
    kvn = kvn_s[...]
    hb = HEADS_PER_STEP
    qall = jnp.dot(qn, wq_ref[...], preferred_element_type=F32)
    knope = jnp.dot(kvn, wk_ref[...], preferred_element_type=F32)
    vt = lax.dot_general(wvt_ref[...], kvn, (((1,), (1,)), ((), ())),
                         preferred_element_type=F32)
    for h in range(hb):
        vt_ref[0, h] = vt[h * V_DIM:(h + 1) * V_DIM, :].astype(BF16)

    c4 = c4_ref[...]
    s4 = s4_ref[...]
    kpe = kpe_ref[...].astype(F32)
    ss_pe = 0.5 * jnp.sum(kpe * kpe, axis=-1, keepdims=True)
    krope = _rope_pair(kpe * gk4_ref[...], c4, s4)

    lane = lax.broadcasted_iota(jnp.int32, (1, LANES), 1)
    even = ((lane // HALF) % 2) == 0
    scale = 1.0 / math.sqrt(QK_DIM)
    gqn = gqn_ref[...]
    gkn = gkn_ref[...]
    for p in range(hb // 2):
        base = hb * NOPE + p * LANES
        qr = qall[:, base:base + LANES]
        qrope = _rope_pair(qr * gq4_ref[...], c4, s4)
        qr2 = qr * qr
        ss_even = jnp.sum(jnp.where(even, qr2, 0.0), axis=-1, keepdims=True)
        ss_odd = jnp.sum(jnp.where(even, 0.0, qr2), axis=-1, keepdims=True)
        for e in range(2):
            h = 2 * p + e
            qh = qall[:, h * NOPE:(h + 1) * NOPE]
            ss = jnp.sum(qh * qh, axis=-1, keepdims=True) + (ss_even if e == 0 else ss_odd)
            rq = lax.rsqrt(ss * (1.0 / QK_DIM) + EPS) * scale
            q_ref[0, h, :, 0:NOPE] = (qh * rq * gqn).astype(BF16)
            keep = even if e == 0 else jnp.logical_not(even)
            q_ref[0, h, :, NOPE:2 * NOPE] = (jnp.where(keep, qrope, 0.0) * rq).astype(BF16)
            kh = knope[:, h * NOPE:(h + 1) * NOPE]
            ssk = jnp.sum(kh * kh, axis=-1, keepdims=True) + ss_pe
            rk = lax.rsqrt(ssk * (1.0 / QK_DIM) + EPS)
            k_ref[0, h, :, 0:NOPE] = (kh * rk * gkn).astype(BF16)
            k_ref[0, h, :, NOPE:2 * NOPE] = (krope * rk).astype(BF16)


def _qkv(z, col, gqa, gkva, wq, wk, wvt, gqn, gq4, gkn, gk4, c4, s4, B, S, tm=512):
    T = z.shape[0]
    hb = HEADS_PER_STEP
    ng = N_HEADS // hb
    spt = S // tm
    qw = hb * QK_DIM
    vec = lambda n: pl.BlockSpec((1, n), lambda i, g: (0, 0))
    qk_spec = pl.BlockSpec((1, hb, tm, 2 * NOPE), lambda i, g: (i // spt, g, i % spt, 0))
    return pl.pallas_call(
        _qkv_kernel,
        grid=(T // tm, ng),
        in_specs=[
            pl.BlockSpec((tm, Q_RANK), lambda i, g: (i, col["q_c"] // Q_RANK)),
            pl.BlockSpec((tm, KV_RANK), lambda i, g: (i, col["kv_c"] // KV_RANK)),
            pl.BlockSpec((tm, LANES), lambda i, g: (i, col["k_pe"] // LANES)),
            vec(Q_RANK), vec(KV_RANK),
            pl.BlockSpec((Q_RANK, qw), lambda i, g: (0, g)),
            pl.BlockSpec((KV_RANK, hb * NOPE), lambda i, g: (0, g)),
            pl.BlockSpec((hb * V_DIM, KV_RANK), lambda i, g: (g, 0)),
            vec(LANES), vec(LANES), vec(LANES), vec(LANES),
            pl.BlockSpec((tm, LANES), lambda i, g: (i, 0)),
            pl.BlockSpec((tm, LANES), lambda i, g: (i, 0)),
        ],
        out_specs=[qk_spec, qk_spec,
                   pl.BlockSpec((1, hb, V_DIM, tm), lambda i, g: (i // spt, g, 0, i % spt))],
        out_shape=[jax.ShapeDtypeStruct((B, N_HEADS, S, 2 * NOPE), BF16),
                   jax.ShapeDtypeStruct((B, N_HEADS, S, 2 * NOPE), BF16),
                   jax.ShapeDtypeStruct((B, N_HEADS, V_DIM, S), BF16)],
        scratch_shapes=[pltpu.VMEM((tm, Q_RANK), BF16), pltpu.VMEM((tm, KV_RANK), BF16)],
        compiler_params=_cparams(("parallel", "arbitrary")),
        name="qkv",
    )(z, z, z, gqa, gkva, wq, wk, wvt, gqn, gq4, gkn, gk4, c4, s4)


def _attn_kernel(q_ref, k_ref, vt_ref, g_ref, o_ref, *, blk):
    S = q_ref.shape[2]
    nq = S // blk
    row = lax.broadcasted_iota(jnp.int32, (blk, blk), 0)
    colq = lax.broadcasted_iota(jnp.int32, (blk, blk), 1)
    causal = row <= colq
    nt = (((1,), (1,)), ((), ()))
    for h in range(HEADS_PER_STEP):
        for j in range(nq):
            qj = q_ref[0, h, j * blk:(j + 1) * blk, :]
            m = jnp.full((1, blk), -jnp.inf, F32)
            l = jnp.zeros((1, blk), F32)
            acc = jnp.zeros((V_DIM, blk), F32)
            for i in range(j + 1):
                ki = k_ref[0, h, i * blk:(i + 1) * blk, :]
                s = lax.dot_general(ki, qj, nt, preferred_element_type=F32)
                if i == j:
                    s = jnp.where(causal, s, -jnp.inf)
                m_new = jnp.maximum(m, jnp.max(s, axis=0, keepdims=True))
                alpha = jnp.exp(m - m_new)
                p = jnp.exp(s - m_new)
                l = alpha * l + jnp.sum(p, axis=0, keepdims=True)
                pv = jnp.dot(vt_ref[0, h, :, i * blk:(i + 1) * blk], p.astype(BF16),
                             preferred_element_type=F32)
                acc = alpha * acc + pv
                m = m_new
            out = (acc * (1.0 / l)).T
            g = g_ref[j * blk:(j + 1) * blk, h * V_DIM:(h + 1) * V_DIM].astype(F32)
            o_ref[j * blk:(j + 1) * blk, h * V_DIM:(h + 1) * V_DIM] = (
                out * (g * jax.nn.sigmoid(g))).astype(BF16)


def _attn(q, k, vt, z, col, blk=512):
    B, H, S, E = q.shape
    hb = HEADS_PER_STEP
    ng = H // hb
    gw = hb * V_DIM
    gcol = col["g_att"] // gw
    qk_spec = pl.BlockSpec((1, hb, S, E), lambda b, g: (b, g, 0, 0))
    return pl.pallas_call(
        functools.partial(_attn_kernel, blk=blk),
        grid=(B, ng),
        in_specs=[qk_spec, qk_spec,
                  pl.BlockSpec((1, hb, V_DIM, S), lambda b, g: (b, g, 0, 0)),
                  pl.BlockSpec((S, gw), lambda b, g: (b, gcol + g))],
        out_specs=pl.BlockSpec((S, gw), lambda b, g: (b, g)),
        out_shape=jax.ShapeDtypeStruct((B * S, H * V_DIM), BF16),
        compiler_params=_cparams(("parallel", "parallel")),
        name="attn",
    )(q, k, vt, z)


def _conv_kernel(a_ref, b_ref, ah_ref, bh_ref, gc_ref, w_ref, bias_ref, lng_ref, lnb_ref,
                 o_ref, u_s, y_s, *, ts, tiles_per_seq):
    r = pl.program_id(0)
    C = a_ref.shape[1]
    first = (r % tiles_per_seq) == 0

    a = a_ref[...].astype(F32)
    b = b_ref[...].astype(F32)
    u_s[HALO:, :] = a * jax.nn.sigmoid(b)
    ah = ah_ref[...].astype(F32)
    bh = bh_ref[...].astype(F32)
    u_s[0:HALO, :] = jnp.where(first, 0.0, ah * jax.nn.sigmoid(bh))

    rows = 128
    off = HALO - (CONV_K - 1)

    def lane_tile(c, carry):
        cs = pl.ds(pl.multiple_of(c * LANES, LANES), LANES)
        for r0 in range(0, ts, rows):
            acc = jnp.zeros((rows, LANES), F32) + bias_ref[:, cs]
            for k in range(CONV_K):
                acc = acc + u_s[r0 + off + k:r0 + off + k + rows, cs] * w_ref[k:k + 1, cs]
            y_s[r0:r0 + rows, cs] = acc
        return carry

    lax.fori_loop(0, C // LANES, lane_tile, 0)

    y = y_s[...]
    mu = jnp.mean(y, axis=-1, keepdims=True)
    yc = y - mu
    var = jnp.mean(yc * yc, axis=-1, keepdims=True)
    yn = yc * lax.rsqrt(var + EPS) * lng_ref[...] + lnb_ref[...]
    gc = gc_ref[...].astype(F32)
    o_ref[...] = (yn * jax.nn.sigmoid(yn) * (gc * jax.nn.sigmoid(gc))).astype(BF16)


def _conv(z, col, w_dw, b_dw, ln_g, ln_b, S, ts=256):
    T = z.shape[0]
    C = w_dw.shape[1]
    ca = col["u_conv"] // C
    cb = ca + 1
    cg = col["g_conv"] // C
    hs = ts // HALO
    main = lambda cidx: pl.BlockSpec((ts, C), lambda r: (r, cidx))
    halo = lambda cidx: pl.BlockSpec((HALO, C), lambda r: (jnp.maximum(r * hs - 1, 0), cidx))
    vec = pl.BlockSpec((1, C), lambda r: (0, 0))
    return pl.pallas_call(
        functools.partial(_conv_kernel, ts=ts, tiles_per_seq=S // ts),
        grid=(T // ts,),
        in_specs=[main(ca), main(cb), halo(ca), halo(cb), main(cg),
                  pl.BlockSpec((CONV_K, C), lambda r: (0, 0)), vec, vec, vec],
        out_specs=pl.BlockSpec((ts, C), lambda r: (r, 0)),
        out_shape=jax.ShapeDtypeStruct((T, C), BF16),
        scratch_shapes=[pltpu.VMEM((ts + HALO, C), F32), pltpu.VMEM((ts, C), F32)],
        compiler_params=_cparams(("parallel",)),
        name="conv",
    )(z, z, z, z, z, w_dw, b_dw, ln_g, ln_b)


def _out_kernel(att_ref, cv_ref, wa_ref, wc_ref, x_ref, o_ref):
    y = jnp.dot(att_ref[...], wa_ref[...], preferred_element_type=F32)
    y = y + jnp.dot(cv_ref[...], wc_ref[...], preferred_element_type=F32)
    o_ref[...] = x_ref[...] + y


def _out_proj(att, cv, w_out, x, tm=1024, tn=512):
    T, A = att.shape
    Cw = cv.shape[1]
    N = w_out.shape[1]
    return pl.pallas_call(
        _out_kernel,
        grid=(T // tm, N // tn),
        in_specs=[pl.BlockSpec((tm, A), lambda i, j: (i, 0)),
                  pl.BlockSpec((tm, Cw), lambda i, j: (i, 0)),
                  pl.BlockSpec((A, tn), lambda i, j: (0, j)),
                  pl.BlockSpec((Cw, tn), lambda i, j: (A // Cw, j)),
                  pl.BlockSpec((tm, tn), lambda i, j: (i, j))],
        out_specs=pl.BlockSpec((tm, tn), lambda i, j: (i, j)),
        out_shape=jax.ShapeDtypeStruct((T, N), F32),
        compiler_params=_cparams(("parallel", "arbitrary")),
        name="out_proj",
    )(att, cv, w_out, w_out, x)


def _in_proj_layout(d_att, d_conv):
    sizes = {"q_c": Q_RANK, "kv_c": KV_RANK, "k_pe": ROPE, "g_att": d_att,
             "u_conv": 2 * d_conv, "g_conv": d_conv}
    src, o = {}, 0
    for name in ("q_c", "kv_c", "k_pe", "g_att", "u_conv", "g_conv"):
        src[name] = o
        o += sizes[name]
    col, o = {}, 0
    for name in ("g_att", "u_conv", "g_conv", "q_c", "kv_c"):
        col[name] = o
        o += sizes[name]
    col["k_pe"] = o
    return sizes, src, col


def _prep_w_in(w_in, sizes, src):
    seg = lambda n: w_in[:, :, src[n]:src[n] + sizes[n]]
    pe = seg("k_pe")
    p1, p2 = pe[..., :HALF], pe[..., HALF:]
    pad = jnp.zeros(w_in.shape[:2] + (LANES,), w_in.dtype)
    return jnp.concatenate([seg("g_att"), seg("u_conv"), seg("g_conv"), seg("q_c"), seg("kv_c"),
                            p1, p1, p2, p2, pad], axis=-1).astype(BF16)


def _prep_wq(w_q_up):
    L, R, _ = w_q_up.shape
    hb = HEADS_PER_STEP
    w = w_q_up.reshape(L, R, N_HEADS // hb, hb, QK_DIM)
    nope = w[..., :NOPE].reshape(L, R, N_HEADS // hb, hb * NOPE)
    x1 = w[..., NOPE:NOPE + HALF].reshape(L, R, N_HEADS // hb, hb // 2, 2, HALF)
    x2 = w[..., NOPE + HALF:].reshape(L, R, N_HEADS // hb, hb // 2, 2, HALF)
    rope = jnp.concatenate([x1, x2], axis=-2).reshape(L, R, N_HEADS // hb, hb * ROPE)
    return jnp.concatenate([nope, rope], axis=-1).reshape(L, R, N_HEADS * QK_DIM).astype(BF16)


def _pair_lanes(g):
    g1, g2 = g[..., :HALF], g[..., HALF:]
    return jnp.concatenate([g1, g1, g2, g2], axis=-1)


def kernel(x, positions, ln_g, w_in, q_a_norm, w_q_up, kv_a_norm, w_kv_up, q_norm, k_norm,
           w_dw, b_dw, conv_ln_g, conv_ln_b, w_out):
    B, S, D = x.shape
    L = w_in.shape[0]
    T = B * S
    d_att = N_HEADS * V_DIM
    d_conv = w_dw.shape[-1]
    sizes, src, col = _in_proj_layout(d_att, d_conv)

    w_in_p = _prep_w_in(w_in, sizes, src)
    wq_p = _prep_wq(w_q_up)
    wkv = w_kv_up.reshape(L, KV_RANK, N_HEADS, NOPE + V_DIM)
    wk_p = wkv[..., :NOPE].reshape(L, KV_RANK, N_HEADS * NOPE).astype(BF16)
    wvt_p = jnp.swapaxes(wkv[..., NOPE:].reshape(L, KV_RANK, N_HEADS * V_DIM), 1, 2).astype(BF16)
    w_out_p = w_out.astype(BF16)

    inv_freq = ROPE_THETA ** (-jnp.arange(HALF, dtype=F32) / HALF)
    ang = positions.astype(F32).reshape(T, 1) * inv_freq
    cos, sin = jnp.cos(ang), jnp.sin(ang)
    c4 = jnp.concatenate([cos, cos, cos, cos], axis=-1)
    s4 = jnp.concatenate([-sin, -sin, sin, sin], axis=-1)

    xf = x.reshape(T, D)
    for l in range(L):
        h = _norm(xf, ln_g[l][None])
        z = _in_proj(h, w_in_p[l])
        q, k, vt = _qkv(z, col, q_a_norm[l][None], kv_a_norm[l][None], wq_p[l], wk_p[l], wvt_p[l],
                        q_norm[l][None, :NOPE], _pair_lanes(q_norm[l][None, NOPE:]),
                        k_norm[l][None, :NOPE], _pair_lanes(k_norm[l][None, NOPE:]),
                        c4, s4, B, S)
        att = _attn(q, k, vt, z, col)
        cv = _conv(z, col, w_dw[l], b_dw[l][None], conv_ln_g[l][None], conv_ln_b[l][None], S)
        xf = _out_proj(att, cv, w_out_p[l], xf)
    return xf.reshape(B, S, D)
```

```python
import functools
import math

import jax
import jax.numpy as jnp
from jax import lax
from jax.experimental import pallas as pl
from jax.experimental.pallas import tpu as pltpu

F32 = jnp.float32
BF16 = jnp.bfloat16

N_HEADS = 16
NOPE = 128
ROPE = 64
HALF = ROPE // 2
QK_DIM = NOPE + ROPE
V_DIM = 128
Q_RANK = 1024
KV_RANK = 512
CONV_K = 31
ROPE_THETA = 10000.0
EPS = 1e-6

LANES = 128
SUBLANES = 8
VMEM_LIMIT = 56 * 1024 * 1024

HEADS_PER_STEP = 4
HALO = 32
LAT_COLS = Q_RANK + KV_RANK + 2 * LANES


def _cparams(sem):
    return pltpu.CompilerParams(dimension_semantics=sem, vmem_limit_bytes=VMEM_LIMIT)


def _sigmoid(x):
    return 0.5 * jnp.tanh(0.5 * x) + 0.5


def _silu(x):
    h = 0.5 * x
    return h * jnp.tanh(h) + h


def _norm_kernel(x_ref, g_ref, o_ref):
    x = x_ref[...]
    rinv = lax.rsqrt(jnp.mean(x * x, axis=-1, keepdims=True) + EPS)
    o_ref[...] = (x * rinv * g_ref[...]).astype(BF16)


def _norm(x, g, tm=256):
    T, D = x.shape
    return pl.pallas_call(
        _norm_kernel,
        grid=(T // tm,),
        in_specs=[pl.BlockSpec((tm, D), lambda i: (i, 0)),
                  pl.BlockSpec((1, D), lambda i: (0, 0))],
        out_specs=pl.BlockSpec((tm, D), lambda i: (i, 0)),
        out_shape=jax.ShapeDtypeStruct((T, D), BF16),
        compiler_params=_cparams(("parallel",)),
        name="norm",
    )(x, g)


def _lat_kernel(a_ref, b_ref, o_ref):
    o_ref[...] = jnp.dot(a_ref[...], b_ref[...], preferred_element_type=F32).astype(BF16)


def _gate_kernel(a_ref, b_ref, o_ref):
    y = jnp.dot(a_ref[...], b_ref[...], preferred_element_type=F32)
    o_ref[...] = _silu(y).astype(BF16)


def _glu_kernel(a_ref, wa_ref, wb_ref, o_ref):
    ya = jnp.dot(a_ref[...], wa_ref[...], preferred_element_type=F32)
    yb = jnp.dot(a_ref[...], wb_ref[...], preferred_element_type=F32)
    o_ref[...] = (ya * _sigmoid(yb)).astype(BF16)


def _in_lat(h, w, tm=512):
    T, K = h.shape
    N = w.shape[1]
    return pl.pallas_call(
        _lat_kernel,
        grid=(T // tm,),
        in_specs=[pl.BlockSpec((tm, K), lambda i: (i, 0)),
                  pl.BlockSpec((K, N), lambda i: (0, 0))],
        out_specs=pl.BlockSpec((tm, N), lambda i: (i, 0)),
        out_shape=jax.ShapeDtypeStruct((T, N), BF16),
        compiler_params=_cparams(("parallel",)),
        name="in_lat",
    )(h, w)


def _in_gate(h, w_rest, d_att, d_conv, tm=1024, tn=1024):
    T, K = h.shape
    n_att = d_att // tn
    skip = (2 * d_conv) // tn
    return pl.pallas_call(
        _gate_kernel,
        grid=(T // tm, (d_att + d_conv) // tn),
        in_specs=[pl.BlockSpec((tm, K), lambda i, j: (i, 0)),
                  pl.BlockSpec((K, tn), lambda i, j: (0, jnp.where(j < n_att, j, j + skip)))],
        out_specs=pl.BlockSpec((tm, tn), lambda i, j: (i, j)),
        out_shape=jax.ShapeDtypeStruct((T, d_att + d_conv), BF16),
        compiler_params=_cparams(("parallel", "arbitrary")),
        name="in_gate",
    )(h, w_rest)


def _in_glu(h, w_rest, d_att, d_conv, tm=1024, tn=512):
    T, K = h.shape
    ca = d_att // tn
    cb = (d_att + d_conv) // tn
    return pl.pallas_call(
        _glu_kernel,
        grid=(T // tm, d_conv // tn),
        in_specs=[pl.BlockSpec((tm, K), lambda i, j: (i, 0)),
                  pl.BlockSpec((K, tn), lambda i, j: (0, ca + j)),
                  pl.BlockSpec((K, tn), lambda i, j: (0, cb + j))],
        out_specs=pl.BlockSpec((tm, tn), lambda i, j: (i, j)),
        out_shape=jax.ShapeDtypeStruct((T, d_conv), BF16),
        compiler_params=_cparams(("parallel", "arbitrary")),
        name="in_glu",
    )(h, w_rest, w_rest)


def _rope_pair(x, c4, s4):
    return x * c4 + pltpu.roll(x, 2 * HALF, axis=1) * s4


def _qkv_kernel(qc_ref, kvc_ref, kpe_ref, gqa_ref, gkva_ref, wq_ref, wk_ref, wvt_ref,
                gqn_ref, gq4_ref, gkn_ref, gk4_ref, c4_ref, s4_ref,
                q_ref, k_ref, vt_ref, qn_s, kvn_s, *, q_scale):
    hg = pl.program_id(1)

    @pl.when(hg == 0)
    def _():
        qc = qc_ref[...].astype(F32)
        r = lax.rsqrt(jnp.mean(qc * qc, axis=-1, keepdims=True) + EPS)
        qn_s[...] = (qc * r * gqa_ref[...]).astype(BF16)
        kvc = kvc_ref[...].astype(F32)
        r = lax.rsqrt(jnp.mean(kvc * kvc, axis=-1, keepdims=True) + EPS)
        kvn_s[...] = (kvc * r * gkva_ref[...]).astype(BF16)

    qn = qn_s[...]
    kvn = kvn_s[...]
    hb = HEADS_PER_STEP
    qall = jnp.dot(qn, wq_ref[...], preferred_element_type=F32)
    knope = jnp.dot(kvn, wk_ref[...], preferred_element_type=F32)
    vt = lax.dot_general(wvt_ref[...], kvn, (((1,), (1,)), ((), ())),
                         preferred_element_type=F32)
    for h in range(hb):
        vt_ref[0, h] = vt[h * V_DIM:(h + 1) * V_DIM, :].astype(BF16)

    src = lax.broadcasted_iota(jnp.int32, (2 * LANES, LANES), 0)
    is_rope = src >= LANES
    src_even = ((src // HALF) % 2) == 0
    red_even = jnp.where(jnp.logical_and(is_rope, jnp.logical_not(src_even)), 0.0, 1.0).astype(BF16)
    red_odd = jnp.where(jnp.logical_and(is_rope, src_even), 0.0, 1.0).astype(BF16)
    red_kpe = jnp.where(is_rope, 0.5, 1.0).astype(BF16)

    def sumsq(nope, rope, red):
        sq = jnp.concatenate([(nope * nope).astype(BF16), (rope * rope).astype(BF16)], axis=1)
        return jnp.dot(sq, red, preferred_element_type=F32)

    c4 = c4_ref[...]
    s4 = s4_ref[...]
    kpe = kpe_ref[...].astype(F32)
    krope = _rope_pair(kpe * gk4_ref[...], c4, s4)

    lane = lax.broadcasted_iota(jnp.int32, (1, LANES), 1)
    even = ((lane // HALF) % 2) == 0
    gqn = gqn_ref[...] * q_scale
    gkn = gkn_ref[...]
    for p in range(hb // 2):
        base = hb * NOPE + p * LANES
        qr = qall[:, base:base + LANES]
        qrope = _rope_pair(qr * gq4_ref[...], c4, s4) * q_scale
        for e in range(2):
            h = 2 * p + e
            qh = qall[:, h * NOPE:(h + 1) * NOPE]
            rq = lax.rsqrt(sumsq(qh, qr, red_even if e == 0 else red_odd) * (1.0 / QK_DIM) + EPS)
            q_ref[0, h, :, 0:NOPE] = (qh * rq * gqn).astype(BF16)
            keep = even if e == 0 else jnp.logical_not(even)
            q_ref[0, h, :, NOPE:2 * NOPE] = (jnp.where(keep, qrope, 0.0) * rq).astype(BF16)
            kh = knope[:, h * NOPE:(h + 1) * NOPE]
            rk = lax.rsqrt(sumsq(kh, kpe, red_kpe) * (1.0 / QK_DIM) + EPS)
            k_ref[0, h, :, 0:NOPE] = (kh * rk * gkn).astype(BF16)
            k_ref[0, h, :, NOPE:2 * NOPE] = (krope * rk).astype(BF16)


def _qkv(za, gqa, gkva, wq, wk, wvt, gqn, gq4, gkn, gk4, c4, s4, B, S, q_scale, tm=512):
    T = za.shape[0]
    hb = HEADS_PER_STEP
    ng = N_HEADS // hb
    spt = S // tm
    qw = hb * QK_DIM
    vec = lambda n: pl.BlockSpec((1, n), lambda i, g: (0, 0))
    qk_spec = pl.BlockSpec((1, hb, tm, 2 * NOPE), lambda i, g: (i // spt, g, i % spt, 0))
    return pl.pallas_call(
        functools.partial(_qkv_kernel, q_scale=q_scale),
        grid=(T // tm, ng),
        in_specs=[
            pl.BlockSpec((tm, Q_RANK), lambda i, g: (i, 0)),
            pl.BlockSpec((tm, KV_RANK), lambda i, g: (i, Q_RANK // KV_RANK)),
            pl.BlockSpec((tm, LANES), lambda i, g: (i, (Q_RANK + KV_RANK) // LANES)),
            vec(Q_RANK), vec(KV_RANK),
            pl.BlockSpec((Q_RANK, qw), lambda i, g: (0, g)),
            pl.BlockSpec((KV_RANK, hb * NOPE), lambda i, g: (0, g)),
            pl.BlockSpec((hb * V_DIM, KV_RANK), lambda i, g: (g, 0)),
            vec(LANES), vec(LANES), vec(LANES), vec(LANES),
            pl.BlockSpec((tm, LANES), lambda i, g: (i, 0)),
            pl.BlockSpec((tm, LANES), lambda i, g: (i, 0)),
        ],
        out_specs=[qk_spec, qk_spec,
                   pl.BlockSpec((1, hb, V_DIM, tm), lambda i, g: (i // spt, g, 0, i % spt))],
        out_shape=[jax.ShapeDtypeStruct((B, N_HEADS, S, 2 * NOPE), BF16),
                   jax.ShapeDtypeStruct((B, N_HEADS, S, 2 * NOPE), BF16),
                   jax.ShapeDtypeStruct((B, N_HEADS, V_DIM, S), BF16)],
        scratch_shapes=[pltpu.VMEM((tm, Q_RANK), BF16), pltpu.VMEM((tm, KV_RANK), BF16)],
        compiler_params=_cparams(("parallel", "arbitrary")),
        name="qkv",
    )(za, za, za, gqa, gkva, wq, wk, wvt, gqn, gq4, gkn, gk4, c4, s4)


def _attn_kernel(q_ref, k_ref, vt_ref, g_ref, o_ref, *, blk):
    S = q_ref.shape[2]
    nq = S // blk
    row = lax.broadcasted_iota(jnp.int32, (blk, blk), 0)
    colq = lax.broadcasted_iota(jnp.int32, (blk, blk), 1)
    causal = row <= colq
    nt = (((1,), (1,)), ((), ()))
    for h in range(HEADS_PER_STEP):
        for j in range(nq):
            qj = q_ref[0, h, j * blk:(j + 1) * blk, :]
            m = jnp.full((1, blk), -jnp.inf, F32)
            l = jnp.zeros((1, blk), F32)
            acc = jnp.zeros((V_DIM, blk), F32)
            for i in range(j + 1):
                ki = k_ref[0, h, i * blk:(i + 1) * blk, :]
                s = lax.dot_general(ki, qj, nt, preferred_element_type=F32)
                if i == j:
                    s = jnp.where(causal, s, -jnp.inf)
                m_new = jnp.maximum(m, jnp.max(s, axis=0, keepdims=True))
                alpha = jnp.exp2(m - m_new)
                p = jnp.exp2(s - m_new)
                l = alpha * l + jnp.sum(p, axis=0, keepdims=True)
                pv = jnp.dot(vt_ref[0, h, :, i * blk:(i + 1) * blk], p.astype(BF16),
                             preferred_element_type=F32)
                acc = alpha * acc + pv
                m = m_new
            out = (acc * (1.0 / l)).T
            g = g_ref[j * blk:(j + 1) * blk, h * V_DIM:(h + 1) * V_DIM].astype(F32)
            o_ref[j * blk:(j + 1) * blk, h * V_DIM:(h + 1) * V_DIM] = (out * g).astype(BF16)


def _attn(q, k, vt, zg, blk=512):
    B, H, S, E = q.shape
    hb = HEADS_PER_STEP
    ng = H // hb
    gw = hb * V_DIM
    qk_spec = pl.BlockSpec((1, hb, S, E), lambda b, g: (b, g, 0, 0))
    return pl.pallas_call(
        functools.partial(_attn_kernel, blk=blk),
        grid=(B, ng),
        in_specs=[qk_spec, qk_spec,
                  pl.BlockSpec((1, hb, V_DIM, S), lambda b, g: (b, g, 0, 0)),
                  pl.BlockSpec((S, gw), lambda b, g: (b, g))],
        out_specs=pl.BlockSpec((S, gw), lambda b, g: (b, g)),
        out_shape=jax.ShapeDtypeStruct((B * S, H * V_DIM), BF16),
        compiler_params=_cparams(("parallel", "parallel")),
        name="attn",
    )(q, k, vt, zg)


def _conv_kernel(u_ref, uh_ref, sg_ref, w_ref, bias_ref, lng_ref, lnb_ref,
                 o_ref, u_s, y_s, *, ts, tiles_per_seq):
    r = pl.program_id(0)
    C = u_ref.shape[1]
    first = (r % tiles_per_seq) == 0
    u_s[HALO:, :] = u_ref[...].astype(F32)
    u_s[0:HALO, :] = jnp.where(first, 0.0, uh_ref[...].astype(F32))

    rows = 128
    off = HALO - (CONV_K - 1)
    max_o = off + CONV_K - 1

    def lane_tile(c, carry):
        cs = pl.ds(pl.multiple_of(c * LANES, LANES), LANES)
        for r0 in range(0, ts, rows):
            acc = jnp.zeros((rows, LANES), F32) + bias_ref[:, cs]
            for s in range(SUBLANES):
                groups = [a for a in range(max_o // SUBLANES + 1)
                          if off <= SUBLANES * a + s <= max_o]
                n = rows if s == 0 else rows + SUBLANES
                z = None
                for a in groups:
                    k = SUBLANES * a + s - off
                    base = r0 + SUBLANES * a
                    t = u_s[base:base + n, cs] * w_ref[k:k + 1, cs]
                    z = t if z is None else z + t
                acc = acc + z[s:s + rows]
            y_s[r0:r0 + rows, cs] = acc
        return carry

    lax.fori_loop(0, C // LANES, lane_tile, 0)

    y = y_s[...]
    mu = jnp.mean(y, axis=-1, keepdims=True)
    yc = y - mu
    var = jnp.mean(yc * yc, axis=-1, keepdims=True)
    yn = yc * lax.rsqrt(var + EPS) * lng_ref[...] + lnb_ref[...]
    o_ref[...] = (_silu(yn) * sg_ref[...].astype(F32)).astype(BF16)


def _conv(u, zg, d_att, w_dw, b_dw, ln_g, ln_b, S, ts=256):
    T, C = u.shape
    hs = ts // HALO
    vec = pl.BlockSpec((1, C), lambda r: (0, 0))
    return pl.pallas_call(
        functools.partial(_conv_kernel, ts=ts, tiles_per_seq=S // ts),
        grid=(T // ts,),
        in_specs=[pl.BlockSpec((ts, C), lambda r: (r, 0)),
                  pl.BlockSpec((HALO, C), lambda r: (jnp.maximum(r * hs - 1, 0), 0)),
                  pl.BlockSpec((ts, C), lambda r: (r, d_att // C)),
                  pl.BlockSpec((CONV_K, C), lambda r: (0, 0)), vec, vec, vec],
        out_specs=pl.BlockSpec((ts, C), lambda r: (r, 0)),
        out_shape=jax.ShapeDtypeStruct((T, C), BF16),
        scratch_shapes=[pltpu.VMEM((ts + HALO, C), F32), pltpu.VMEM((ts, C), F32)],
        compiler_params=_cparams(("parallel",)),
        name="conv",
    )(u, u, zg, w_dw, b_dw, ln_g, ln_b)


def _out_kernel(att_ref, cv_ref, wa_ref, wc_ref, x_ref, o_ref):
    y = jnp.dot(att_ref[...], wa_ref[...], preferred_element_type=F32)
    y = y + jnp.dot(cv_ref[...], wc_ref[...], preferred_element_type=F32)
    o_ref[...] = x_ref[...] + y


def _out_proj(att, cv, w_out, x, tm=1024, tn=512):
    T, A = att.shape
    Cw = cv.shape[1]
    N = w_out.shape[1]
    return pl.pallas_call(
        _out_kernel,
        grid=(T // tm, N // tn),
        in_specs=[pl.BlockSpec((tm, A), lambda i, j: (i, 0)),
                  pl.BlockSpec((tm, Cw), lambda i, j: (i, 0)),
                  pl.BlockSpec((A, tn), lambda i, j: (0, j)),
                  pl.BlockSpec((Cw, tn), lambda i, j: (A // Cw, j)),
                  pl.BlockSpec((tm, tn), lambda i, j: (i, j))],
        out_specs=pl.BlockSpec((tm, tn), lambda i, j: (i, j)),
        out_shape=jax.ShapeDtypeStruct((T, N), F32),
        compiler_params=_cparams(("parallel", "arbitrary")),
        name="out_proj",
    )(att, cv, w_out, w_out, x)


def _prep_w_lat(w_in_l):
    lat = Q_RANK + KV_RANK
    pe = w_in_l[:, lat:lat + ROPE]
    p1, p2 = pe[:, :HALF], pe[:, HALF:]
    pad = jnp.zeros((w_in_l.shape[0], LANES), w_in_l.dtype)
    return jnp.concatenate([w_in_l[:, :lat], p1, p1, p2, p2, pad], axis=-1).astype(BF16)


def _prep_wq(w_q_up):
    L, R, _ = w_q_up.shape
    hb = HEADS_PER_STEP
    w = w_q_up.reshape(L, R, N_HEADS // hb, hb, QK_DIM)
    nope = w[..., :NOPE].reshape(L, R, N_HEADS // hb, hb * NOPE)
    x1 = w[..., NOPE:NOPE + HALF].reshape(L, R, N_HEADS // hb, hb // 2, 2, HALF)
    x2 = w[..., NOPE + HALF:].reshape(L, R, N_HEADS // hb, hb // 2, 2, HALF)
    rope = jnp.concatenate([x1, x2], axis=-2).reshape(L, R, N_HEADS // hb, hb * ROPE)
    return jnp.concatenate([nope, rope], axis=-1).reshape(L, R, N_HEADS * QK_DIM).astype(BF16)


def _pair_lanes(g):
    g1, g2 = g[..., :HALF], g[..., HALF:]
    return jnp.concatenate([g1, g1, g2, g2], axis=-1)


def kernel(x, positions, ln_g, w_in, q_a_norm, w_q_up, kv_a_norm, w_kv_up, q_norm, k_norm,
           w_dw, b_dw, conv_ln_g, conv_ln_b, w_out):
    B, S, D = x.shape
    L = w_in.shape[0]
    T = B * S
    d_att = N_HEADS * V_DIM
    d_conv = w_dw.shape[-1]
    rest0 = Q_RANK + KV_RANK + ROPE

    wq_p = _prep_wq(w_q_up)
    wkv = w_kv_up.reshape(L, KV_RANK, N_HEADS, NOPE + V_DIM)
    wk_p = wkv[..., :NOPE].reshape(L, KV_RANK, N_HEADS * NOPE).astype(BF16)
    wvt_p = jnp.swapaxes(wkv[..., NOPE:].reshape(L, KV_RANK, N_HEADS * V_DIM), 1, 2).astype(BF16)

    inv_freq = ROPE_THETA ** (-jnp.arange(HALF, dtype=F32) / HALF)
    ang = positions.astype(F32).reshape(T, 1) * inv_freq
    cos, sin = jnp.cos(ang), jnp.sin(ang)
    c4 = jnp.concatenate([cos, cos, cos, cos], axis=-1)
    s4 = jnp.concatenate([-sin, -sin, sin, sin], axis=-1)
    q_scale = math.log2(math.e) / math.sqrt(QK_DIM)

    xf = x.reshape(T, D)
    for l in range(L):
        w_lat = _prep_w_lat(w_in[l])
        w_rest = w_in[l, :, rest0:].astype(BF16)
        w_out_l = w_out[l].astype(BF16)
        h = _norm(xf, ln_g[l][None])
        za = _in_lat(h, w_lat)
        zg = _in_gate(h, w_rest, d_att, d_conv)
        u = _in_glu(h, w_rest, d_att, d_conv)
        q, k, vt = _qkv(za, q_a_norm[l][None], kv_a_norm[l][None], wq_p[l], wk_p[l], wvt_p[l],
                        q_norm[l][None, :NOPE], _pair_lanes(q_norm[l][None, NOPE:]),
                        k_norm[l][None, :NOPE], _pair_lanes(k_norm[l][None, NOPE:]),
                        c4, s4, B, S, q_scale)
        att = _attn(q, k, vt, zg)
        cv = _conv(u, zg, d_att, w_dw[l], b_dw[l][None], conv_ln_g[l][None], conv_ln_b[l][None], S)
        xf = _out_proj(att, cv, w_out_l, xf)
    return xf.reshape(B, S, D)
```

```python
import functools
import math

import jax
import jax.numpy as jnp
from jax import lax
from jax.experimental import pallas as pl
from jax.experimental.pallas import tpu as pltpu

F32 = jnp.float32
BF16 = jnp.bfloat16

N_HEADS = 16
NOPE = 128
ROPE = 64
HALF = ROPE // 2
QK_DIM = NOPE + ROPE
V_DIM = 128
Q_RANK = 1024
KV_RANK = 512
LAT = Q_RANK + KV_RANK
CONV_K = 31
ROPE_THETA = 10000.0
EPS = 1e-6

LANES = 128
SUBLANES = 8
VMEM_LIMIT = 56 * 1024 * 1024

HEADS_PER_STEP = 4
HALO = 32


def _cparams(sem):
    return pltpu.CompilerParams(dimension_semantics=sem, vmem_limit_bytes=VMEM_LIMIT)


def _sigmoid(x):
    return 0.5 * jnp.tanh(0.5 * x) + 0.5


def _silu(x):
    h = 0.5 * x
    return h * jnp.tanh(h) + h


def _w_cast_kernel(w_ref, g_ref, o_ref):
    o_ref[...] = (w_ref[...] * g_ref[...]).astype(BF16)


def _w_shift_kernel(w_ref, nxt_ref, g_ref, o_ref, *, shift):
    bw = w_ref.shape[1]
    x = jnp.concatenate([w_ref[...], nxt_ref[...]], axis=1)
    x = pltpu.roll(x, x.shape[1] - shift, axis=1)[:, :bw]
    o_ref[...] = (x * g_ref[...]).astype(BF16)


def _w_pe_kernel(w_ref, g_ref, o_ref):
    x = w_ref[...]
    lane = lax.broadcasted_iota(jnp.int32, (1, LANES), 1)
    r1 = pltpu.roll(x, HALF, axis=1)
    r2 = pltpu.roll(x, 2 * HALF, axis=1)
    y = jnp.where(lane < HALF, x, jnp.where(lane < 3 * HALF, r1, r2))
    o_ref[...] = (y * g_ref[...]).astype(BF16)


def _w_q_kernel(w_ref, o_ref):
    x = w_ref[...]
    hb = HEADS_PER_STEP
    parts = [x[:, h * QK_DIM:h * QK_DIM + NOPE] for h in range(hb)]

    def rp(h, half):
        s = h * QK_DIM + NOPE + half * HALF
        return x[:, s:s + HALF]

    for p in range(hb // 2):
        parts += [rp(2 * p, 0), rp(2 * p + 1, 0), rp(2 * p, 1), rp(2 * p + 1, 1)]
    o_ref[...] = jnp.concatenate(parts, axis=1).astype(BF16)


def _prep_w_in(w_in_l, g_col, d_rest, tr=2048, bw=512):
    K = w_in_l.shape[0]
    gspec = lambda nd: pl.BlockSpec((tr, 1), (lambda i, j: (i, 0)) if nd == 2 else (lambda i: (i, 0)))
    w_lat = pl.pallas_call(
        _w_cast_kernel,
        grid=(K // tr, LAT // bw),
        in_specs=[pl.BlockSpec((tr, bw), lambda i, j: (i, j)), gspec(2)],
        out_specs=pl.BlockSpec((tr, bw), lambda i, j: (i, j)),
        out_shape=jax.ShapeDtypeStruct((K, LAT), BF16),
        compiler_params=_cparams(("parallel", "parallel")),
        name="w_cast",
    )(w_in_l, g_col)
    w_pe = pl.pallas_call(
        _w_pe_kernel,
        grid=(K // tr,),
        in_specs=[pl.BlockSpec((tr, LANES), lambda i: (i, LAT // LANES)), gspec(1)],
        out_specs=pl.BlockSpec((tr, LANES), lambda i: (i, 0)),
        out_shape=jax.ShapeDtypeStruct((K, LANES), BF16),
        compiler_params=_cparams(("parallel",)),
        name="w_pe",
    )(w_in_l, g_col)
    nb = bw // LANES
    w_rest = pl.pallas_call(
        functools.partial(_w_shift_kernel, shift=ROPE),
        grid=(K // tr, d_rest // bw),
        in_specs=[pl.BlockSpec((tr, bw), lambda i, j: (i, LAT // bw + j)),
                  pl.BlockSpec((tr, LANES), lambda i, j: (i, LAT // LANES + nb * (j + 1))),
                  gspec(2)],
        out_specs=pl.BlockSpec((tr, bw), lambda i, j: (i, j)),
        out_shape=jax.ShapeDtypeStruct((K, d_rest), BF16),
        compiler_params=_cparams(("parallel", "parallel")),
        name="w_shift",
    )(w_in_l, w_in_l, g_col)
    return w_lat, w_pe, w_rest


def _prep_w_q(w_q_l):
    R, N = w_q_l.shape
    gw = HEADS_PER_STEP * QK_DIM
    return pl.pallas_call(
        _w_q_kernel,
        grid=(N // gw,),
        in_specs=[pl.BlockSpec((R, gw), lambda g: (0, g))],
        out_specs=pl.BlockSpec((R, gw), lambda g: (0, g)),
        out_shape=jax.ShapeDtypeStruct((R, N), BF16),
        compiler_params=_cparams(("parallel",)),
        name="w_q",
    )(w_q_l)


def _norm0_kernel(x_ref, xb_ref, rinv_ref):
    x = x_ref[...]
    xb_ref[...] = x.astype(BF16)
    rinv_ref[...] = lax.rsqrt(jnp.mean(x * x, axis=-1, keepdims=True) + EPS)


def _norm0(x, tm=256):
    T, D = x.shape
    return pl.pallas_call(
        _norm0_kernel,
        grid=(T // tm,),
        in_specs=[pl.BlockSpec((tm, D), lambda i: (i, 0))],
        out_specs=[pl.BlockSpec((tm, D), lambda i: (i, 0)),
                   pl.BlockSpec((tm, 1), lambda i: (i, 0))],
        out_shape=[jax.ShapeDtypeStruct((T, D), BF16), jax.ShapeDtypeStruct((T, 1), F32)],
        compiler_params=_cparams(("parallel",)),
        name="norm0",
    )(x)


def _lat_kernel(a_ref, r_ref, w_ref, wpe_ref, o_ref):
    a = a_ref[...]
    r = r_ref[...]
    n = w_ref.shape[1]
    o_ref[:, 0:n] = (jnp.dot(a, w_ref[...], preferred_element_type=F32) * r).astype(BF16)
    o_ref[:, n:] = (jnp.dot(a, wpe_ref[...], preferred_element_type=F32) * r).astype(BF16)


def _gate_kernel(a_ref, r_ref, b_ref, o_ref):
    y = jnp.dot(a_ref[...], b_ref[...], preferred_element_type=F32) * r_ref[...]
    o_ref[...] = _silu(y).astype(BF16)


def _glu_kernel(a_ref, r_ref, wa_ref, wb_ref, o_ref):
    r = r_ref[...]
    ya = jnp.dot(a_ref[...], wa_ref[...], preferred_element_type=F32) * r
    yb = jnp.dot(a_ref[...], wb_ref[...], preferred_element_type=F32) * r
    o_ref[...] = (ya * _sigmoid(yb)).astype(BF16)


def _in_lat(xb, rinv, w_lat, w_pe, tm=512):
    T, K = xb.shape
    N = w_lat.shape[1] + w_pe.shape[1]
    return pl.pallas_call(
        _lat_kernel,
        grid=(T // tm,),
        in_specs=[pl.BlockSpec((tm, K), lambda i: (i, 0)),
                  pl.BlockSpec((tm, 1), lambda i: (i, 0)),
                  pl.BlockSpec(w_lat.shape, lambda i: (0, 0)),
                  pl.BlockSpec(w_pe.shape, lambda i: (0, 0))],
        out_specs=pl.BlockSpec((tm, N), lambda i: (i, 0)),
        out_shape=jax.ShapeDtypeStruct((T, N), BF16),
        compiler_params=_cparams(("parallel",)),
        name="in_lat",
    )(xb, rinv, w_lat, w_pe)


def _in_gate(xb, rinv, w_rest, d_att, d_conv, tm=1024, tn=1024):
    T, K = xb.shape
    n_att = d_att // tn
    skip = (2 * d_conv) // tn
    return pl.pallas_call(
        _gate_kernel,
        grid=(T // tm, (d_att + d_conv) // tn),
        in_specs=[pl.BlockSpec((tm, K), lambda i, j: (i, 0)),
                  pl.BlockSpec((tm, 1), lambda i, j: (i, 0)),
                  pl.BlockSpec((K, tn), lambda i, j: (0, jnp.where(j < n_att, j, j + skip)))],
        out_specs=pl.BlockSpec((tm, tn), lambda i, j: (i, j)),
        out_shape=jax.ShapeDtypeStruct((T, d_att + d_conv), BF16),
        compiler_params=_cparams(("parallel", "arbitrary")),
        name="in_gate",
    )(xb, rinv, w_rest)


def _in_glu(xb, rinv, w_rest, d_att, d_conv, tm=1024, tn=512):
    T, K = xb.shape
    ca = d_att // tn
    cb = (d_att + d_conv) // tn
    return pl.pallas_call(
        _glu_kernel,
        grid=(T // tm, d_conv // tn),
        in_specs=[pl.BlockSpec((tm, K), lambda i, j: (i, 0)),
                  pl.BlockSpec((tm, 1), lambda i, j: (i, 0)),
                  pl.BlockSpec((K, tn), lambda i, j: (0, ca + j)),
                  pl.BlockSpec((K, tn), lambda i, j: (0, cb + j))],
        out_specs=pl.BlockSpec((tm, tn), lambda i, j: (i, j)),
        out_shape=jax.ShapeDtypeStruct((T, d_conv), BF16),
        compiler_params=_cparams(("parallel", "arbitrary")),
        name="in_glu",
    )(xb, rinv, w_rest, w_rest)


def _rope_pair(x, c4, s4):
    return x * c4 + pltpu.roll(x, 2 * HALF, axis=1) * s4


def _qkv_kernel(qc_ref, kvc_ref, kpe_ref, gqa_ref, gkva_ref, wq_ref, wk_ref, wvt_ref,
                gqn_ref, gq4_ref, gkn_ref, gk4_ref, c4_ref, s4_ref,
                q_ref, k_ref, vt_ref, qn_s, kvn_s, *, q_scale):
    hg = pl.program_id(1)

    @pl.when(hg == 0)
    def _():
        qc = qc_ref[...].astype(F32)
        r = lax.rsqrt(jnp.mean(qc * qc, axis=-1, keepdims=True) + EPS)
        qn_s[...] = (qc * r * gqa_ref[...]).astype(BF16)
        kvc = kvc_ref[...].astype(F32)
        r = lax.rsqrt(jnp.mean(kvc * kvc, axis=-1, keepdims=True) + EPS)
        kvn_s[...] = (kvc * r * gkva_ref[...]).astype(BF16)

    qn = qn_s[...]
    kvn = kvn_s[...]
    hb = HEADS_PER_STEP
    qall = jnp.dot(qn, wq_ref[...], preferred_element_type=F32)
    knope = jnp.dot(kvn, wk_ref[...], preferred_element_type=F32)
    vt = lax.dot_general(wvt_ref[...], kvn, (((1,), (1,)), ((), ())),
                         preferred_element_type=F32)
    for h in range(hb):
        vt_ref[0, h] = vt[h * V_DIM:(h + 1) * V_DIM, :].astype(BF16)

    src = lax.broadcasted_iota(jnp.int32, (2 * LANES, LANES), 0)
    is_rope = src >= LANES
    src_even = ((src // HALF) % 2) == 0
    red_even = jnp.where(jnp.logical_and(is_rope, jnp.logical_not(src_even)), 0.0, 1.0).astype(BF16)
    red_odd = jnp.where(jnp.logical_and(is_rope, src_even), 0.0, 1.0).astype(BF16)
    red_kpe = jnp.where(is_rope, 0.5, 1.0).astype(BF16)

    def sumsq(nope, rope, red):
        sq = jnp.concatenate([(nope * nope).astype(BF16), (rope * rope).astype(BF16)], axis=1)
        return jnp.dot(sq, red, preferred_element_type=F32)

    c4 = c4_ref[...]
    s4 = s4_ref[...]
    kpe = kpe_ref[...].astype(F32)
    krope = _rope_pair(kpe * gk4_ref[...], c4, s4)

    lane = lax.broadcasted_iota(jnp.int32, (1, LANES), 1)
    even = ((lane // HALF) % 2) == 0
    gqn = gqn_ref[...] * q_scale
    gkn = gkn_ref[...]
    for p in range(hb // 2):
        base = hb * NOPE + p * LANES
        qr = qall[:, base:base + LANES]
        qrope = _rope_pair(qr * gq4_ref[...], c4, s4) * q_scale
        for e in range(2):
            h = 2 * p + e
            qh = qall[:, h * NOPE:(h + 1) * NOPE]
            rq = lax.rsqrt(sumsq(qh, qr, red_even if e == 0 else red_odd) * (1.0 / QK_DIM) + EPS)
            q_ref[0, h, :, 0:NOPE] = (qh * rq * gqn).astype(BF16)
            keep = even if e == 0 else jnp.logical_not(even)
            q_ref[0, h, :, NOPE:2 * NOPE] = (jnp.where(keep, qrope, 0.0) * rq).astype(BF16)
            kh = knope[:, h * NOPE:(h + 1) * NOPE]
            rk = lax.rsqrt(sumsq(kh, kpe, red_kpe) * (1.0 / QK_DIM) + EPS)
            k_ref[0, h, :, 0:NOPE] = (kh * rk * gkn).astype(BF16)
            k_ref[0, h, :, NOPE:2 * NOPE] = (krope * rk).astype(BF16)


def _qkv(za, gqa, gkva, wq, wk, wvt, gqn, gq4, gkn, gk4, c4, s4, B, S, q_scale, tm=512):
    T = za.shape[0]
    hb = HEADS_PER_STEP
    ng = N_HEADS // hb
    spt = S // tm
    qw = hb * QK_DIM
    vec = lambda n: pl.BlockSpec((1, n), lambda i, g: (0, 0))
    qk_spec = pl.BlockSpec((1, hb, tm, 2 * NOPE), lambda i, g: (i // spt, g, i % spt, 0))
    return pl.pallas_call(
        functools.partial(_qkv_kernel, q_scale=q_scale),
        grid=(T // tm, ng),
        in_specs=[
            pl.BlockSpec((tm, Q_RANK), lambda i, g: (i, 0)),
            pl.BlockSpec((tm, KV_RANK), lambda i, g: (i, Q_RANK // KV_RANK)),
            pl.BlockSpec((tm, LANES), lambda i, g: (i, LAT // LANES)),
            vec(Q_RANK), vec(KV_RANK),
            pl.BlockSpec((Q_RANK, qw), lambda i, g: (0, g)),
            pl.BlockSpec((KV_RANK, hb * NOPE), lambda i, g: (0, g)),
            pl.BlockSpec((hb * V_DIM, KV_RANK), lambda i, g: (g, 0)),
            vec(LANES), vec(LANES), vec(LANES), vec(LANES),
            pl.BlockSpec((tm, LANES), lambda i, g: (i, 0)),
            pl.BlockSpec((tm, LANES), lambda i, g: (i, 0)),
        ],
        out_specs=[qk_spec, qk_spec,
                   pl.BlockSpec((1, hb, V_DIM, tm), lambda i, g: (i // spt, g, 0, i % spt))],
        out_shape=[jax.ShapeDtypeStruct((B, N_HEADS, S, 2 * NOPE), BF16),
                   jax.ShapeDtypeStruct((B, N_HEADS, S, 2 * NOPE), BF16),
                   jax.ShapeDtypeStruct((B, N_HEADS, V_DIM, S), BF16)],
        scratch_shapes=[pltpu.VMEM((tm, Q_RANK), BF16), pltpu.VMEM((tm, KV_RANK), BF16)],
        compiler_params=_cparams(("parallel", "arbitrary")),
        name="qkv",
    )(za, za, za, gqa, gkva, wq, wk, wvt, gqn, gq4, gkn, gk4, c4, s4)


def _attn_kernel(q_ref, k_ref, vt_ref, g_ref, o_ref, *, blk):
    S = q_ref.shape[2]
    nq = S // blk
    row = lax.broadcasted_iota(jnp.int32, (blk, blk), 0)
    colq = lax.broadcasted_iota(jnp.int32, (blk, blk), 1)
    causal = row <= colq
    nt = (((1,), (1,)), ((), ()))
    for h in range(HEADS_PER_STEP):
        for j in range(nq):
            lo = j * blk
            hi = lo + blk
            qj = q_ref[0, h, lo:hi, :]
            sd = lax.dot_general(k_ref[0, h, lo:hi, :], qj, nt, preferred_element_type=F32)
            sd = jnp.where(causal, sd, -jnp.inf)
            m = jnp.max(sd, axis=0, keepdims=True)
            if j > 0:
                sf = lax.dot_general(k_ref[0, h, 0:lo, :], qj, nt, preferred_element_type=F32)
                m = jnp.maximum(m, jnp.max(sf, axis=0, keepdims=True))
            pd = jnp.exp2(sd - m)
            l = jnp.sum(pd, axis=0, keepdims=True)
            acc = jnp.dot(vt_ref[0, h, :, lo:hi], pd.astype(BF16), preferred_element_type=F32)
            if j > 0:
                pf = jnp.exp2(sf - m)
                l = l + jnp.sum(pf, axis=0, keepdims=True)
                acc = acc + jnp.dot(vt_ref[0, h, :, 0:lo], pf.astype(BF16),
                                    preferred_element_type=F32)
            out = (acc * (1.0 / l)).T
            g = g_ref[lo:hi, h * V_DIM:(h + 1) * V_DIM].astype(F32)
            o_ref[lo:hi, h * V_DIM:(h + 1) * V_DIM] = (out * g).astype(BF16)


def _attn(q, k, vt, zg, blk=1024):
    B, H, S, E = q.shape
    hb = HEADS_PER_STEP
    ng = H // hb
    gw = hb * V_DIM
    qk_spec = pl.BlockSpec((1, hb, S, E), lambda b, g: (b, g, 0, 0))
    return pl.pallas_call(
        functools.partial(_attn_kernel, blk=blk),
        grid=(B, ng),
        in_specs=[qk_spec, qk_spec,
                  pl.BlockSpec((1, hb, V_DIM, S), lambda b, g: (b, g, 0, 0)),
                  pl.BlockSpec((S, gw), lambda b, g: (b, g))],
        out_specs=pl.BlockSpec((S, gw), lambda b, g: (b, g)),
        out_shape=jax.ShapeDtypeStruct((B * S, H * V_DIM), BF16),
        compiler_params=_cparams(("parallel", "parallel")),
        name="attn",
    )(q, k, vt, zg)


def _conv_kernel(u_ref, uh_ref, sg_ref, w_ref, bias_ref, lng_ref, lnb_ref,
                 o_ref, u_s, y_s, *, ts, tiles_per_seq):
    r = pl.program_id(0)
    C = u_ref.shape[1]
    first = (r % tiles_per_seq) == 0
    u_s[HALO:, :] = u_ref[...].astype(F32)
    u_s[0:HALO, :] = jnp.where(first, 0.0, uh_ref[...].astype(F32))

    rows = 128
    off = HALO - (CONV_K - 1)
    max_o = off + CONV_K - 1

    def lane_tile(c, carry):
        cs = pl.ds(pl.multiple_of(c * LANES, LANES), LANES)
        for r0 in range(0, ts, rows):
            acc = jnp.zeros((rows, LANES), F32) + bias_ref[:, cs]
            for s in range(SUBLANES):
                groups = [a for a in range(max_o // SUBLANES + 1)
                          if off <= SUBLANES * a + s <= max_o]
                n = rows if s == 0 else rows + SUBLANES
                z = None
                for a in groups:
                    k = SUBLANES * a + s - off
                    base = r0 + SUBLANES * a
                    t = u_s[base:base + n, cs] * w_ref[k:k + 1, cs]
                    z = t if z is None else z + t
                acc = acc + z[s:s + rows]
            y_s[r0:r0 + rows, cs] = acc
        return carry

    lax.fori_loop(0, C // LANES, lane_tile, 0)

    y = y_s[...]
    mu = jnp.mean(y, axis=-1, keepdims=True)
    yc = y - mu
    var = jnp.mean(yc * yc, axis=-1, keepdims=True)
    yn = yc * lax.rsqrt(var + EPS) * lng_ref[...] + lnb_ref[...]
    o_ref[...] = (_silu(yn) * sg_ref[...].astype(F32)).astype(BF16)


def _conv(u, zg, d_att, w_dw, b_dw, ln_g, ln_b, S, ts=256):
    T, C = u.shape
    hs = ts // HALO
    vec = pl.BlockSpec((1, C), lambda r: (0, 0))
    return pl.pallas_call(
        functools.partial(_conv_kernel, ts=ts, tiles_per_seq=S // ts),
        grid=(T // ts,),
        in_specs=[pl.BlockSpec((ts, C), lambda r: (r, 0)),
                  pl.BlockSpec((HALO, C), lambda r: (jnp.maximum(r * hs - 1, 0), 0)),
                  pl.BlockSpec((ts, C), lambda r: (r, d_att // C)),
                  pl.BlockSpec((CONV_K, C), lambda r: (0, 0)), vec, vec, vec],
        out_specs=pl.BlockSpec((ts, C), lambda r: (r, 0)),
        out_shape=jax.ShapeDtypeStruct((T, C), BF16),
        scratch_shapes=[pltpu.VMEM((ts + HALO, C), F32), pltpu.VMEM((ts, C), F32)],
        compiler_params=_cparams(("parallel",)),
        name="conv",
    )(u, u, zg, w_dw, b_dw, ln_g, ln_b)


def _out_kernel(att_ref, cv_ref, wa_ref, wc_ref, x_ref, o_ref, xb_ref, rinv_ref, ss_s, *, n_total):
    j = pl.program_id(1)
    y = jnp.dot(att_ref[...], wa_ref[...], preferred_element_type=F32)
    y = y + jnp.dot(cv_ref[...], wc_ref[...], preferred_element_type=F32)
    xn = x_ref[...] + y
    o_ref[...] = xn
    xb_ref[...] = xn.astype(BF16)
    part = jnp.sum(xn * xn, axis=-1, keepdims=True)

    @pl.when(j == 0)
    def _():
        ss_s[...] = part

    @pl.when(j > 0)
    def _():
        ss_s[...] = ss_s[...] + part

    @pl.when(j == pl.num_programs(1) - 1)
    def _():
        rinv_ref[...] = lax.rsqrt(ss_s[...] * (1.0 / n_total) + EPS)


def _out_proj(att, cv, w_out, x, tm=1024, tn=512):
    T, A = att.shape
    Cw = cv.shape[1]
    N = w_out.shape[1]
    return pl.pallas_call(
        functools.partial(_out_kernel, n_total=N),
        grid=(T // tm, N // tn),
        in_specs=[pl.BlockSpec((tm, A), lambda i, j: (i, 0)),
                  pl.BlockSpec((tm, Cw), lambda i, j: (i, 0)),
                  pl.BlockSpec((A, tn), lambda i, j: (0, j)),
                  pl.BlockSpec((Cw, tn), lambda i, j: (A // Cw, j)),
                  pl.BlockSpec((tm, tn), lambda i, j: (i, j))],
        out_specs=[pl.BlockSpec((tm, tn), lambda i, j: (i, j)),
                   pl.BlockSpec((tm, tn), lambda i, j: (i, j)),
                   pl.BlockSpec((tm, 1), lambda i, j: (i, 0))],
        out_shape=[jax.ShapeDtypeStruct((T, N), F32), jax.ShapeDtypeStruct((T, N), BF16),
                   jax.ShapeDtypeStruct((T, 1), F32)],
        scratch_shapes=[pltpu.VMEM((tm, 1), F32)],
        compiler_params=_cparams(("parallel", "arbitrary")),
        name="out_proj",
    )(att, cv, w_out, w_out, x)


def _pair_lanes(g):
    g1, g2 = g[..., :HALF], g[..., HALF:]
    return jnp.concatenate([g1, g1, g2, g2], axis=-1)


def kernel(x, positions, ln_g, w_in, q_a_norm, w_q_up, kv_a_norm, w_kv_up, q_norm, k_norm,
           w_dw, b_dw, conv_ln_g, conv_ln_b, w_out):
    B, S, D = x.shape
    L = w_in.shape[0]
    T = B * S
    d_att = N_HEADS * V_DIM
    d_conv = w_dw.shape[-1]
    d_rest = d_att + 3 * d_conv

    wkv = w_kv_up.reshape(L, KV_RANK, N_HEADS, NOPE + V_DIM)
    wk_p = wkv[..., :NOPE].reshape(L, KV_RANK, N_HEADS * NOPE).astype(BF16)
    wvt_p = jnp.swapaxes(wkv[..., NOPE:].reshape(L, KV_RANK, N_HEADS * V_DIM), 1, 2).astype(BF16)

    inv_freq = ROPE_THETA ** (-jnp.arange(HALF, dtype=F32) / HALF)
    ang = positions.astype(F32).reshape(T, 1) * inv_freq
    cos, sin = jnp.cos(ang), jnp.sin(ang)
    c4 = jnp.concatenate([cos, cos, cos, cos], axis=-1)
    s4 = jnp.concatenate([-sin, -sin, sin, sin], axis=-1)
    q_scale = math.log2(math.e) / math.sqrt(QK_DIM)

    xf = x.reshape(T, D)
    xb, rinv = _norm0(xf)
    for l in range(L):
        w_lat, w_pe, w_rest = _prep_w_in(w_in[l], ln_g[l][:, None], d_rest)
        wq_l = _prep_w_q(w_q_up[l])
        za = _in_lat(xb, rinv, w_lat, w_pe)
        zg = _in_gate(xb, rinv, w_rest, d_att, d_conv)
        u = _in_glu(xb, rinv, w_rest, d_att, d_conv)
        q, k, vt = _qkv(za, q_a_norm[l][None], kv_a_norm[l][None], wq_l, wk_p[l], wvt_p[l],
                        q_norm[l][None, :NOPE], _pair_lanes(q_norm[l][None, NOPE:]),
                        k_norm[l][None, :NOPE], _pair_lanes(k_norm[l][None, NOPE:]),
                        c4, s4, B, S, q_scale)
        att = _attn(q, k, vt, zg)
        cv = _conv(u, zg, d_att, w_dw[l], b_dw[l][None], conv_ln_g[l][None], conv_ln_b[l][None], S)
        xf, xb, rinv = _out_proj(att, cv, w_out[l].astype(BF16), xf)
    return xf.reshape(B, S, D)
```

```python
import functools
import math

import jax
import jax.numpy as jnp
from jax import lax
from jax.experimental import pallas as pl
from jax.experimental.pallas import tpu as pltpu

F32 = jnp.float32
BF16 = jnp.bfloat16

N_HEADS = 16
NOPE = 128
ROPE = 64
HALF = ROPE // 2
QK_DIM = NOPE + ROPE
V_DIM = 128
Q_RANK = 1024
KV_RANK = 512
LAT = Q_RANK + KV_RANK
CONV_K = 31
ROPE_THETA = 10000.0
EPS = 1e-6

LANES = 128
SUBLANES = 8
VMEM_LIMIT = 56 * 1024 * 1024

HEADS_PER_STEP = 4
HALO = 32


def _cparams(sem):
    return pltpu.CompilerParams(dimension_semantics=sem, vmem_limit_bytes=VMEM_LIMIT)


def _sigmoid(x):
    return 0.5 * jnp.tanh(0.5 * x) + 0.5


def _silu(x):
    h = 0.5 * x
    return h * jnp.tanh(h) + h


def _w_cast_kernel(w_ref, g_ref, o_ref):
    o_ref[...] = (w_ref[...] * g_ref[:, 0:1]).astype(BF16)


def _w_shift_kernel(w_ref, nxt_ref, g_ref, o_ref, *, shift):
    bw = w_ref.shape[1]
    x = jnp.concatenate([w_ref[...], nxt_ref[...]], axis=1)
    x = pltpu.roll(x, x.shape[1] - shift, axis=1)[:, :bw]
    o_ref[...] = (x * g_ref[:, 0:1]).astype(BF16)


def _w_pe_kernel(w_ref, g_ref, o_ref):
    x = w_ref[...]
    lane = lax.broadcasted_iota(jnp.int32, (1, LANES), 1)
    r1 = pltpu.roll(x, HALF, axis=1)
    r2 = pltpu.roll(x, 2 * HALF, axis=1)
    y = jnp.where(lane < HALF, x, jnp.where(lane < 3 * HALF, r1, r2))
    o_ref[...] = (y * g_ref[:, 0:1]).astype(BF16)


def _w_q_kernel(w_ref, o_ref):
    x = w_ref[...]
    hb = HEADS_PER_STEP
    parts = [x[:, h * QK_DIM:h * QK_DIM + NOPE] for h in range(hb)]

    def rp(h, half):
        s = h * QK_DIM + NOPE + half * HALF
        return x[:, s:s + HALF]

    for p in range(hb // 2):
        parts += [rp(2 * p, 0), rp(2 * p + 1, 0), rp(2 * p, 1), rp(2 * p + 1, 1)]
    o_ref[...] = jnp.concatenate(parts, axis=1).astype(BF16)


def _prep_w_in(w_in, l, g_col, d_rest, tr=2048, bw=512):
    K = w_in.shape[1]
    gspec = lambda nd: pl.BlockSpec((tr, LANES), (lambda i, j: (i, 0)) if nd == 2 else (lambda i: (i, 0)))
    w_lat = pl.pallas_call(
        _w_cast_kernel,
        grid=(K // tr, LAT // bw),
        in_specs=[pl.BlockSpec((None, tr, bw), lambda i, j: (l, i, j)), gspec(2)],
        out_specs=pl.BlockSpec((tr, bw), lambda i, j: (i, j)),
        out_shape=jax.ShapeDtypeStruct((K, LAT), BF16),
        compiler_params=_cparams(("parallel", "parallel")),
        name="w_cast",
    )(w_in, g_col)
    w_pe = pl.pallas_call(
        _w_pe_kernel,
        grid=(K // tr,),
        in_specs=[pl.BlockSpec((None, tr, LANES), lambda i: (l, i, LAT // LANES)), gspec(1)],
        out_specs=pl.BlockSpec((tr, LANES), lambda i: (i, 0)),
        out_shape=jax.ShapeDtypeStruct((K, LANES), BF16),
        compiler_params=_cparams(("parallel",)),
        name="w_pe",
    )(w_in, g_col)
    nb = bw // LANES
    w_rest = pl.pallas_call(
        functools.partial(_w_shift_kernel, shift=ROPE),
        grid=(K // tr, d_rest // bw),
        in_specs=[pl.BlockSpec((None, tr, bw), lambda i, j: (l, i, LAT // bw + j)),
                  pl.BlockSpec((None, tr, LANES), lambda i, j: (l, i, LAT // LANES + nb * (j + 1))),
                  gspec(2)],
        out_specs=pl.BlockSpec((tr, bw), lambda i, j: (i, j)),
        out_shape=jax.ShapeDtypeStruct((K, d_rest), BF16),
        compiler_params=_cparams(("parallel", "parallel")),
        name="w_shift",
    )(w_in, w_in, g_col)
    return w_lat, w_pe, w_rest


def _prep_w_q(w_q_up, l):
    _, R, N = w_q_up.shape
    gw = HEADS_PER_STEP * QK_DIM
    return pl.pallas_call(
        _w_q_kernel,
        grid=(N // gw,),
        in_specs=[pl.BlockSpec((None, R, gw), lambda g: (l, 0, g))],
        out_specs=pl.BlockSpec((R, gw), lambda g: (0, g)),
        out_shape=jax.ShapeDtypeStruct((R, N), BF16),
        compiler_params=_cparams(("parallel",)),
        name="w_q",
    )(w_q_up)


def _norm0_kernel(x_ref, xb_ref, rinv_ref):
    x = x_ref[...]
    xb_ref[...] = x.astype(BF16)
    rinv = lax.rsqrt(jnp.mean(x * x, axis=-1, keepdims=True) + EPS)
    rinv_ref[...] = jnp.broadcast_to(rinv, rinv_ref.shape)


def _norm0(x, tm=512):
    T, D = x.shape
    return pl.pallas_call(
        _norm0_kernel,
        grid=(T // tm,),
        in_specs=[pl.BlockSpec((tm, D), lambda i: (i, 0))],
        out_specs=[pl.BlockSpec((tm, D), lambda i: (i, 0)),
                   pl.BlockSpec((tm, LANES), lambda i: (i, 0))],
        out_shape=[jax.ShapeDtypeStruct((T, D), BF16), jax.ShapeDtypeStruct((T, LANES), F32)],
        compiler_params=_cparams(("parallel",)),
        name="norm0",
    )(x)


def _lat_kernel(a_ref, r_ref, w_ref, wpe_ref, o_ref):
    a = a_ref[...]
    r = r_ref[:, 0:1]
    n = w_ref.shape[1]
    o_ref[:, 0:n] = (jnp.dot(a, w_ref[...], preferred_element_type=F32) * r).astype(BF16)
    o_ref[:, n:] = (jnp.dot(a, wpe_ref[...], preferred_element_type=F32) * r).astype(BF16)


def _gate_kernel(a_ref, r_ref, b_ref, o_ref):
    y = jnp.dot(a_ref[...], b_ref[...], preferred_element_type=F32) * r_ref[:, 0:1]
    o_ref[...] = _silu(y).astype(BF16)


def _glu_kernel(a_ref, r_ref, wa_ref, wb_ref, o_ref):
    r = r_ref[:, 0:1]
    ya = jnp.dot(a_ref[...], wa_ref[...], preferred_element_type=F32) * r
    yb = jnp.dot(a_ref[...], wb_ref[...], preferred_element_type=F32) * r
    o_ref[...] = (ya * _sigmoid(yb)).astype(BF16)


def _in_lat(xb, rinv, w_lat, w_pe, tm=512):
    T, K = xb.shape
    N = w_lat.shape[1] + w_pe.shape[1]
    return pl.pallas_call(
        _lat_kernel,
        grid=(T // tm,),
        in_specs=[pl.BlockSpec((tm, K), lambda i: (i, 0)),
                  pl.BlockSpec((tm, LANES), lambda i: (i, 0)),
                  pl.BlockSpec(w_lat.shape, lambda i: (0, 0)),
                  pl.BlockSpec(w_pe.shape, lambda i: (0, 0))],
        out_specs=pl.BlockSpec((tm, N), lambda i: (i, 0)),
        out_shape=jax.ShapeDtypeStruct((T, N), BF16),
        compiler_params=_cparams(("parallel",)),
        name="in_lat",
    )(xb, rinv, w_lat, w_pe)


def _in_gate(xb, rinv, w_rest, d_att, d_conv, tm=1024, tn=1024):
    T, K = xb.shape
    n_att = d_att // tn
    skip = (2 * d_conv) // tn
    return pl.pallas_call(
        _gate_kernel,
        grid=(T // tm, (d_att + d_conv) // tn),
        in_specs=[pl.BlockSpec((tm, K), lambda i, j: (i, 0)),
                  pl.BlockSpec((tm, LANES), lambda i, j: (i, 0)),
                  pl.BlockSpec((K, tn), lambda i, j: (0, jnp.where(j < n_att, j, j + skip)))],
        out_specs=pl.BlockSpec((tm, tn), lambda i, j: (i, j)),
        out_shape=jax.ShapeDtypeStruct((T, d_att + d_conv), BF16),
        compiler_params=_cparams(("parallel", "arbitrary")),
        name="in_gate",
    )(xb, rinv, w_rest)


def _in_glu(xb, rinv, w_rest, d_att, d_conv, tm=1024, tn=512):
    T, K = xb.shape
    ca = d_att // tn
    cb = (d_att + d_conv) // tn
    return pl.pallas_call(
        _glu_kernel,
        grid=(T // tm, d_conv // tn),
        in_specs=[pl.BlockSpec((tm, K), lambda i, j: (i, 0)),
                  pl.BlockSpec((tm, LANES), lambda i, j: (i, 0)),
                  pl.BlockSpec((K, tn), lambda i, j: (0, ca + j)),
                  pl.BlockSpec((K, tn), lambda i, j: (0, cb + j))],
        out_specs=pl.BlockSpec((tm, tn), lambda i, j: (i, j)),
        out_shape=jax.ShapeDtypeStruct((T, d_conv), BF16),
        compiler_params=_cparams(("parallel", "arbitrary")),
        name="in_glu",
    )(xb, rinv, w_rest, w_rest)


def _rope_pair(x, c4, s4):
    return x * c4 + pltpu.roll(x, 2 * HALF, axis=1) * s4


def _qkv_kernel(qc_ref, kvc_ref, kpe_ref, gqa_ref, gkva_ref, wq_ref, wkv_ref,
                gqn_ref, gq4_ref, gkn_ref, gk4_ref, c4_ref, s4_ref,
                q_ref, k_ref, vt_ref, qn_s, kvn_s, *, q_scale):
    hg = pl.program_id(1)

    @pl.when(hg == 0)
    def _():
        qc = qc_ref[...].astype(F32)
        r = lax.rsqrt(jnp.mean(qc * qc, axis=-1, keepdims=True) + EPS)
        qn_s[...] = (qc * r * gqa_ref[...]).astype(BF16)
        kvc = kvc_ref[...].astype(F32)
        r = lax.rsqrt(jnp.mean(kvc * kvc, axis=-1, keepdims=True) + EPS)
        kvn_s[...] = (kvc * r * gkva_ref[...]).astype(BF16)

    qn = qn_s[...]
    kvn = kvn_s[...]
    hb = HEADS_PER_STEP
    qall = jnp.dot(qn, wq_ref[...], preferred_element_type=F32)
    kv = jnp.dot(kvn, wkv_ref[...].astype(BF16),
                 preferred_element_type=F32)
    kvw = NOPE + V_DIM
    for h in range(hb):
        vt_ref[0, h] = kv[:, h * kvw + NOPE:(h + 1) * kvw].T.astype(BF16)

    src = lax.broadcasted_iota(jnp.int32, (2 * LANES, LANES), 0)
    is_rope = src >= LANES
    src_even = ((src // HALF) % 2) == 0
    red_even = jnp.where(jnp.logical_and(is_rope, jnp.logical_not(src_even)), 0.0, 1.0).astype(BF16)
    red_odd = jnp.where(jnp.logical_and(is_rope, src_even), 0.0, 1.0).astype(BF16)
    red_kpe = jnp.where(is_rope, 0.5, 1.0).astype(BF16)

    def sumsq(nope, rope, red):
        sq = jnp.concatenate([(nope * nope).astype(BF16), (rope * rope).astype(BF16)], axis=1)
        return jnp.dot(sq, red, preferred_element_type=F32)

    c4 = c4_ref[...]
    s4 = s4_ref[...]
    kpe = kpe_ref[...].astype(F32)
    krope = _rope_pair(kpe * gk4_ref[...], c4, s4)

    lane = lax.broadcasted_iota(jnp.int32, (1, LANES), 1)
    even = ((lane // HALF) % 2) == 0
    gqn = gqn_ref[...] * q_scale
    gkn = gkn_ref[...]
    for p in range(hb // 2):
        base = hb * NOPE + p * LANES
        qr = qall[:, base:base + LANES]
        qrope = _rope_pair(qr * gq4_ref[...], c4, s4) * q_scale
        for e in range(2):
            h = 2 * p + e
            qh = qall[:, h * NOPE:(h + 1) * NOPE]
            rq = lax.rsqrt(sumsq(qh, qr, red_even if e == 0 else red_odd) * (1.0 / QK_DIM) + EPS)
            q_ref[0, h, :, 0:NOPE] = (qh * rq * gqn).astype(BF16)
            keep = even if e == 0 else jnp.logical_not(even)
            q_ref[0, h, :, NOPE:2 * NOPE] = (jnp.where(keep, qrope, 0.0) * rq).astype(BF16)
            kh = kv[:, h * kvw:h * kvw + NOPE]
            rk = lax.rsqrt(sumsq(kh, kpe, red_kpe) * (1.0 / QK_DIM) + EPS)
            k_ref[0, h, :, 0:NOPE] = (kh * rk * gkn).astype(BF16)
            k_ref[0, h, :, NOPE:2 * NOPE] = (krope * rk).astype(BF16)


def _qkv(za, gqa, gkva, wq, w_kv_up, l, gqn, gq4, gkn, gk4, c4, s4, B, S, q_scale, tm=512):
    T = za.shape[0]
    hb = HEADS_PER_STEP
    ng = N_HEADS // hb
    spt = S // tm
    qw = hb * QK_DIM
    vec = lambda n: pl.BlockSpec((1, n), lambda i, g: (0, 0))
    qk_spec = pl.BlockSpec((1, hb, tm, 2 * NOPE), lambda i, g: (i // spt, g, i % spt, 0))
    return pl.pallas_call(
        functools.partial(_qkv_kernel, q_scale=q_scale),
        grid=(T // tm, ng),
        in_specs=[
            pl.BlockSpec((tm, Q_RANK), lambda i, g: (i, 0)),
            pl.BlockSpec((tm, KV_RANK), lambda i, g: (i, Q_RANK // KV_RANK)),
            pl.BlockSpec((tm, LANES), lambda i, g: (i, LAT // LANES)),
            vec(Q_RANK), vec(KV_RANK),
            pl.BlockSpec((Q_RANK, qw), lambda i, g: (0, g)),
            pl.BlockSpec((None, KV_RANK, hb * (NOPE + V_DIM)), lambda i, g: (l, 0, g)),
            vec(LANES), vec(LANES), vec(LANES), vec(LANES),
            pl.BlockSpec((tm, LANES), lambda i, g: (i, 0)),
            pl.BlockSpec((tm, LANES), lambda i, g: (i, 0)),
        ],
        out_specs=[qk_spec, qk_spec,
                   pl.BlockSpec((1, hb, V_DIM, tm), lambda i, g: (i // spt, g, 0, i % spt))],
        out_shape=[jax.ShapeDtypeStruct((B, N_HEADS, S, 2 * NOPE), BF16),
                   jax.ShapeDtypeStruct((B, N_HEADS, S, 2 * NOPE), BF16),
                   jax.ShapeDtypeStruct((B, N_HEADS, V_DIM, S), BF16)],
        scratch_shapes=[pltpu.VMEM((tm, Q_RANK), BF16), pltpu.VMEM((tm, KV_RANK), BF16)],
        compiler_params=_cparams(("parallel", "arbitrary")),
        name="qkv",
    )(za, za, za, gqa, gkva, wq, w_kv_up, gqn, gq4, gkn, gk4, c4, s4)


def _attn_kernel(q_ref, k_ref, vt_ref, g_ref, o_ref, *, blk):
    S = q_ref.shape[2]
    nq = S // blk
    row = lax.broadcasted_iota(jnp.int32, (blk, blk), 0)
    colq = lax.broadcasted_iota(jnp.int32, (blk, blk), 1)
    causal = row <= colq
    nt = (((1,), (1,)), ((), ()))
    for h in range(HEADS_PER_STEP):
        for j in range(nq):
            lo = j * blk
            hi = lo + blk
            qj = q_ref[0, h, lo:hi, :]
            sd = lax.dot_general(k_ref[0, h, lo:hi, :], qj, nt, preferred_element_type=F32)
            sd = jnp.where(causal, sd, -jnp.inf)
            m = jnp.max(sd, axis=0, keepdims=True)
            if j > 0:
                sf = lax.dot_general(k_ref[0, h, 0:lo, :], qj, nt, preferred_element_type=F32)
                m = jnp.maximum(m, jnp.max(sf, axis=0, keepdims=True))
            pd = jnp.exp2(sd - m)
            l = jnp.sum(pd, axis=0, keepdims=True)
            acc = jnp.dot(vt_ref[0, h, :, lo:hi], pd.astype(BF16), preferred_element_type=F32)
            if j > 0:
                pf = jnp.exp2(sf - m)
                l = l + jnp.sum(pf, axis=0, keepdims=True)
                acc = acc + jnp.dot(vt_ref[0, h, :, 0:lo], pf.astype(BF16),
                                    preferred_element_type=F32)
            out = (acc * (1.0 / l)).T
            g = g_ref[lo:hi, h * V_DIM:(h + 1) * V_DIM].astype(F32)
            o_ref[lo:hi, h * V_DIM:(h + 1) * V_DIM] = (out * g).astype(BF16)


def _attn(q, k, vt, zg, blk=1024):
    B, H, S, E = q.shape
    hb = HEADS_PER_STEP
    ng = H // hb
    gw = hb * V_DIM
    qk_spec = pl.BlockSpec((1, hb, S, E), lambda b, g: (b, g, 0, 0))
    return pl.pallas_call(
        functools.partial(_attn_kernel, blk=blk),
        grid=(B, ng),
        in_specs=[qk_spec, qk_spec,
                  pl.BlockSpec((1, hb, V_DIM, S), lambda b, g: (b, g, 0, 0)),
                  pl.BlockSpec((S, gw), lambda b, g: (b, g))],
        out_specs=pl.BlockSpec((S, gw), lambda b, g: (b, g)),
        out_shape=jax.ShapeDtypeStruct((B * S, H * V_DIM), BF16),
        compiler_params=_cparams(("parallel", "parallel")),
        name="attn",
    )(q, k, vt, zg)


def _conv_kernel(u_ref, uh_ref, sg_ref, w_ref, bias_ref, lng_ref, lnb_ref,
                 o_ref, u_s, y_s, *, ts, tiles_per_seq):
    r = pl.program_id(0)
    C = u_ref.shape[1]
    first = (r % tiles_per_seq) == 0
    u_s[HALO:, :] = u_ref[...].astype(F32)
    u_s[0:HALO, :] = jnp.where(first, 0.0, uh_ref[...].astype(F32))

    rows = 128
    off = HALO - (CONV_K - 1)
    max_o = off + CONV_K - 1

    def lane_tile(c, carry):
        cs = pl.ds(pl.multiple_of(c * LANES, LANES), LANES)
        for r0 in range(0, ts, rows):
            acc = jnp.zeros((rows, LANES), F32) + bias_ref[:, cs]
            for s in range(SUBLANES):
                groups = [a for a in range(max_o // SUBLANES + 1)
                          if off <= SUBLANES * a + s <= max_o]
                n = rows if s == 0 else rows + SUBLANES
                z = None
                for a in groups:
                    k = SUBLANES * a + s - off
                    base = r0 + SUBLANES * a
                    t = u_s[base:base + n, cs] * w_ref[k:k + 1, cs]
                    z = t if z is None else z + t
                acc = acc + z[s:s + rows]
            y_s[r0:r0 + rows, cs] = acc
        return carry

    lax.fori_loop(0, C // LANES, lane_tile, 0)

    y = y_s[...]
    mu = jnp.mean(y, axis=-1, keepdims=True)
    yc = y - mu
    var = jnp.mean(yc * yc, axis=-1, keepdims=True)
    yn = yc * lax.rsqrt(var + EPS) * lng_ref[...] + lnb_ref[...]
    o_ref[...] = (_silu(yn) * sg_ref[...].astype(F32)).astype(BF16)


def _conv(u, zg, d_att, w_dw, b_dw, ln_g, ln_b, S, ts=256):
    T, C = u.shape
    hs = ts // HALO
    vec = pl.BlockSpec((1, C), lambda r: (0, 0))
    return pl.pallas_call(
        functools.partial(_conv_kernel, ts=ts, tiles_per_seq=S // ts),
        grid=(T // ts,),
        in_specs=[pl.BlockSpec((ts, C), lambda r: (r, 0)),
                  pl.BlockSpec((HALO, C), lambda r: (jnp.maximum(r * hs - 1, 0), 0)),
                  pl.BlockSpec((ts, C), lambda r: (r, d_att // C)),
                  pl.BlockSpec((CONV_K, C), lambda r: (0, 0)), vec, vec, vec],
        out_specs=pl.BlockSpec((ts, C), lambda r: (r, 0)),
        out_shape=jax.ShapeDtypeStruct((T, C), BF16),
        scratch_shapes=[pltpu.VMEM((ts + HALO, C), F32), pltpu.VMEM((ts, C), F32)],
        compiler_params=_cparams(("parallel",)),
        name="conv",
    )(u, u, zg, w_dw, b_dw, ln_g, ln_b)


def _out_kernel(att_ref, cv_ref, wa_ref, wc_ref, x_ref, o_ref, xb_ref, rinv_ref, ss_s, *, n_total):
    j = pl.program_id(1)
    y = jnp.dot(att_ref[...], wa_ref[...], preferred_element_type=F32)
    y = y + jnp.dot(cv_ref[...], wc_ref[...], preferred_element_type=F32)
    xn = x_ref[...] + y
    o_ref[...] = xn
    xb_ref[...] = xn.astype(BF16)
    part = jnp.sum(xn * xn, axis=-1, keepdims=True)

    @pl.when(j == 0)
    def _():
        ss_s[...] = part

    @pl.when(j > 0)
    def _():
        ss_s[...] = ss_s[...] + part

    @pl.when(j == pl.num_programs(1) - 1)
    def _():
        rinv = lax.rsqrt(ss_s[...] * (1.0 / n_total) + EPS)
        rinv_ref[...] = jnp.broadcast_to(rinv, rinv_ref.shape)


def _out_proj(att, cv, w_out, l, x, tm=1024, tn=512):
    T, A = att.shape
    Cw = cv.shape[1]
    N = w_out.shape[2]
    return pl.pallas_call(
        functools.partial(_out_kernel, n_total=N),
        grid=(T // tm, N // tn),
        in_specs=[pl.BlockSpec((tm, A), lambda i, j: (i, 0)),
                  pl.BlockSpec((tm, Cw), lambda i, j: (i, 0)),
                  pl.BlockSpec((None, A, tn), lambda i, j: (l, 0, j)),
                  pl.BlockSpec((None, Cw, tn), lambda i, j: (l, A // Cw, j)),
                  pl.BlockSpec((tm, tn), lambda i, j: (i, j))],
        out_specs=[pl.BlockSpec((tm, tn), lambda i, j: (i, j)),
                   pl.BlockSpec((tm, tn), lambda i, j: (i, j)),
                   pl.BlockSpec((tm, LANES), lambda i, j: (i, 0))],
        out_shape=[jax.ShapeDtypeStruct((T, N), F32), jax.ShapeDtypeStruct((T, N), BF16),
                   jax.ShapeDtypeStruct((T, LANES), F32)],
        scratch_shapes=[pltpu.VMEM((tm, 1), F32)],
        compiler_params=_cparams(("parallel", "arbitrary")),
        name="out_proj",
    )(att, cv, w_out, w_out, x)


def _pair_lanes(g):
    g1, g2 = g[..., :HALF], g[..., HALF:]
    return jnp.concatenate([g1, g1, g2, g2], axis=-1)


def kernel(x, positions, ln_g, w_in, q_a_norm, w_q_up, kv_a_norm, w_kv_up, q_norm, k_norm,
           w_dw, b_dw, conv_ln_g, conv_ln_b, w_out):
    B, S, D = x.shape
    L = w_in.shape[0]
    T = B * S
    d_att = N_HEADS * V_DIM
    d_conv = w_dw.shape[-1]
    d_rest = d_att + 3 * d_conv

    w_out_b = w_out.astype(BF16)
    g_cols = jnp.broadcast_to(ln_g[:, :, None], ln_g.shape + (LANES,))

    inv_freq = ROPE_THETA ** (-jnp.arange(HALF, dtype=F32) / HALF)
    ang = positions.astype(F32).reshape(T, 1) * inv_freq
    cos, sin = jnp.cos(ang), jnp.sin(ang)
    c4 = jnp.concatenate([cos, cos, cos, cos], axis=-1)
    s4 = jnp.concatenate([-sin, -sin, sin, sin], axis=-1)
    q_scale = math.log2(math.e) / math.sqrt(QK_DIM)

    xf = x.reshape(T, D)
    xb, rinv = _norm0(xf)
    for l in range(L):
        w_lat, w_pe, w_rest = _prep_w_in(w_in, l, g_cols[l], d_rest)
        wq_l = _prep_w_q(w_q_up, l)
        za = _in_lat(xb, rinv, w_lat, w_pe)
        zg = _in_gate(xb, rinv, w_rest, d_att, d_conv)
        u = _in_glu(xb, rinv, w_rest, d_att, d_conv)
        q, k, vt = _qkv(za, q_a_norm[l][None], kv_a_norm[l][None], wq_l, w_kv_up, l,
                        q_norm[l][None, :NOPE], _pair_lanes(q_norm[l][None, NOPE:]),
                        k_norm[l][None, :NOPE], _pair_lanes(k_norm[l][None, NOPE:]),
                        c4, s4, B, S, q_scale)
        att = _attn(q, k, vt, zg)
        cv = _conv(u, zg, d_att, w_dw[l], b_dw[l][None], conv_ln_g[l][None], conv_ln_b[l][None], S)
        xf, xb, rinv = _out_proj(att, cv, w_out_b, l, xf)
    return xf.reshape(B, S, D)
```

```python
import functools
import math

import jax
import jax.numpy as jnp
from jax import lax
from jax.experimental import pallas as pl
from jax.experimental.pallas import tpu as pltpu

F32 = jnp.float32
BF16 = jnp.bfloat16

N_HEADS = 16
NOPE = 128
ROPE = 64
HALF = ROPE // 2
QK_DIM = NOPE + ROPE
V_DIM = 128
Q_RANK = 1024
KV_RANK = 512
LAT = Q_RANK + KV_RANK
CONV_K = 31
ROPE_THETA = 10000.0
EPS = 1e-6

LANES = 128
SUBLANES = 8
VMEM_LIMIT = 56 * 1024 * 1024

HEADS_PER_STEP = 4
HALO = 32


def _cparams(sem):
    return pltpu.CompilerParams(dimension_semantics=sem, vmem_limit_bytes=VMEM_LIMIT)


def _sigmoid(x):
    return 0.5 * jnp.tanh(0.5 * x) + 0.5


def _silu(x):
    h = 0.5 * x
    return h * jnp.tanh(h) + h


def _w_rows_kernel(w_ref, g_ref, o_ref):
    o_ref[...] = (w_ref[...] * g_ref[...]).astype(BF16)


def _w_pe_kernel(w_ref, g_ref, o_ref):
    x = (w_ref[...] * g_ref[...]).astype(BF16)
    for t, s in enumerate((0, 0, 1, 1)):
        o_ref[t * HALF:(t + 1) * HALF, :] = x[s * HALF:(s + 1) * HALF, :]


def _w_q_kernel(w_ref, o_ref):
    x = w_ref[...]
    hb = HEADS_PER_STEP
    parts = [x[:, h * QK_DIM:h * QK_DIM + NOPE] for h in range(hb)]

    def rp(h, half):
        s = h * QK_DIM + NOPE + half * HALF
        return x[:, s:s + HALF]

    for p in range(hb // 2):
        parts += [rp(2 * p, 0), rp(2 * p + 1, 0), rp(2 * p, 1), rp(2 * p + 1, 1)]
    o_ref[...] = jnp.concatenate(parts, axis=1).astype(BF16)


def _prep_w_in(w_in_t, l, g_row, d_rest, tl=512):
    K = w_in_t.shape[2]
    gspec = pl.BlockSpec((1, K), lambda i: (0, 0))

    def rows(name, n_rows, tr, first_block):
        return pl.pallas_call(
            _w_rows_kernel,
            grid=(n_rows // tr,),
            in_specs=[pl.BlockSpec((None, tr, K), lambda i: (l, first_block + i, 0)), gspec],
            out_specs=pl.BlockSpec((tr, K), lambda i: (i, 0)),
            out_shape=jax.ShapeDtypeStruct((n_rows, K), BF16),
            compiler_params=_cparams(("parallel",)),
            name=name,
        )(w_in_t, g_row)

    w_lat = rows("w_lat", LAT, tl, 0)
    w_rest = rows("w_rest", d_rest, ROPE, (LAT + ROPE) // ROPE)
    w_pe = pl.pallas_call(
        _w_pe_kernel,
        grid=(1,),
        in_specs=[pl.BlockSpec((None, ROPE, K), lambda i: (l, LAT // ROPE, 0)), gspec],
        out_specs=pl.BlockSpec((4 * HALF, K), lambda i: (0, 0)),
        out_shape=jax.ShapeDtypeStruct((4 * HALF, K), BF16),
        compiler_params=_cparams(("arbitrary",)),
        name="w_pe",
    )(w_in_t, g_row)
    return w_lat, w_pe, w_rest


def _prep_w_q(w_q_up, l):
    _, R, N = w_q_up.shape
    gw = HEADS_PER_STEP * QK_DIM
    return pl.pallas_call(
        _w_q_kernel,
        grid=(N // gw,),
        in_specs=[pl.BlockSpec((None, R, gw), lambda g: (l, 0, g))],
        out_specs=pl.BlockSpec((R, gw), lambda g: (0, g)),
        out_shape=jax.ShapeDtypeStruct((R, N), BF16),
        compiler_params=_cparams(("parallel",)),
        name="w_q",
    )(w_q_up)


def _norm0_kernel(x_ref, xb_ref, rinv_ref):
    x = x_ref[...]
    xb_ref[...] = x.astype(BF16)
    rinv = lax.rsqrt(jnp.mean(x * x, axis=-1, keepdims=True) + EPS)
    rinv_ref[...] = jnp.broadcast_to(rinv, rinv_ref.shape)


def _norm0(x, tm=512):
    T, D = x.shape
    return pl.pallas_call(
        _norm0_kernel,
        grid=(T // tm,),
        in_specs=[pl.BlockSpec((tm, D), lambda i: (i, 0))],
        out_specs=[pl.BlockSpec((tm, D), lambda i: (i, 0)),
                   pl.BlockSpec((tm, LANES), lambda i: (i, 0))],
        out_shape=[jax.ShapeDtypeStruct((T, D), BF16), jax.ShapeDtypeStruct((T, LANES), F32)],
        compiler_params=_cparams(("parallel",)),
        name="norm0",
    )(x)


def _dot_nt(a, w_t):
    return lax.dot_general(a, w_t, (((1,), (1,)), ((), ())), preferred_element_type=F32)


def _lat_kernel(a_ref, r_ref, w_ref, wpe_ref, o_ref):
    a = a_ref[...]
    r = r_ref[:, 0:1]
    n = w_ref.shape[0]
    o_ref[:, 0:n] = (_dot_nt(a, w_ref[...]) * r).astype(BF16)
    o_ref[:, n:] = (_dot_nt(a, wpe_ref[...]) * r).astype(BF16)


def _gate_kernel(a_ref, r_ref, b_ref, o_ref):
    y = _dot_nt(a_ref[...], b_ref[...]) * r_ref[:, 0:1]
    o_ref[...] = _silu(y).astype(BF16)


def _glu_kernel(a_ref, r_ref, wa_ref, wb_ref, o_ref):
    r = r_ref[:, 0:1]
    ya = _dot_nt(a_ref[...], wa_ref[...]) * r
    yb = _dot_nt(a_ref[...], wb_ref[...]) * r
    o_ref[...] = (ya * _sigmoid(yb)).astype(BF16)


def _in_lat(xb, rinv, w_lat, w_pe, tm=512):
    T, K = xb.shape
    N = w_lat.shape[0] + w_pe.shape[0]
    return pl.pallas_call(
        _lat_kernel,
        grid=(T // tm,),
        in_specs=[pl.BlockSpec((tm, K), lambda i: (i, 0)),
                  pl.BlockSpec((tm, LANES), lambda i: (i, 0)),
                  pl.BlockSpec(w_lat.shape, lambda i: (0, 0)),
                  pl.BlockSpec(w_pe.shape, lambda i: (0, 0))],
        out_specs=pl.BlockSpec((tm, N), lambda i: (i, 0)),
        out_shape=jax.ShapeDtypeStruct((T, N), BF16),
        compiler_params=_cparams(("parallel",)),
        name="in_lat",
    )(xb, rinv, w_lat, w_pe)


def _in_gate(xb, rinv, w_rest, d_att, d_conv, tm=1024, tn=1024):
    T, K = xb.shape
    n_att = d_att // tn
    skip = (2 * d_conv) // tn
    return pl.pallas_call(
        _gate_kernel,
        grid=(T // tm, (d_att + d_conv) // tn),
        in_specs=[pl.BlockSpec((tm, K), lambda i, j: (i, 0)),
                  pl.BlockSpec((tm, LANES), lambda i, j: (i, 0)),
                  pl.BlockSpec((tn, K), lambda i, j: (jnp.where(j < n_att, j, j + skip), 0))],
        out_specs=pl.BlockSpec((tm, tn), lambda i, j: (i, j)),
        out_shape=jax.ShapeDtypeStruct((T, d_att + d_conv), BF16),
        compiler_params=_cparams(("parallel", "arbitrary")),
        name="in_gate",
    )(xb, rinv, w_rest)


def _in_glu(xb, rinv, w_rest, d_att, d_conv, tm=1024, tn=512):
    T, K = xb.shape
    ca = d_att // tn
    cb = (d_att + d_conv) // tn
    return pl.pallas_call(
        _glu_kernel,
        grid=(T // tm, d_conv // tn),
        in_specs=[pl.BlockSpec((tm, K), lambda i, j: (i, 0)),
                  pl.BlockSpec((tm, LANES), lambda i, j: (i, 0)),
                  pl.BlockSpec((tn, K), lambda i, j: (ca + j, 0)),
                  pl.BlockSpec((tn, K), lambda i, j: (cb + j, 0))],
        out_specs=pl.BlockSpec((tm, tn), lambda i, j: (i, j)),
        out_shape=jax.ShapeDtypeStruct((T, d_conv), BF16),
        compiler_params=_cparams(("parallel", "arbitrary")),
        name="in_glu",
    )(xb, rinv, w_rest, w_rest)


def _rope_pair(x, c4, s4):
    return x * c4 + pltpu.roll(x, 2 * HALF, axis=1) * s4


def _qkv_kernel(qc_ref, kvc_ref, kpe_ref, gqa_ref, gkva_ref, wq_ref, wkv_ref,
                gqn_ref, gq4_ref, gkn_ref, gk4_ref, c4_ref, s4_ref,
                q_ref, k_ref, vt_ref, qn_s, kvn_s, *, q_scale):
    hg = pl.program_id(1)

    @pl.when(hg == 0)
    def _():
        qc = qc_ref[...].astype(F32)
        r = lax.rsqrt(jnp.mean(qc * qc, axis=-1, keepdims=True) + EPS)
        qn_s[...] = (qc * r * gqa_ref[...]).astype(BF16)
        kvc = kvc_ref[...].astype(F32)
        r = lax.rsqrt(jnp.mean(kvc * kvc, axis=-1, keepdims=True) + EPS)
        kvn_s[...] = (kvc * r * gkva_ref[...]).astype(BF16)

    qn = qn_s[...]
    kvn = kvn_s[...]
    hb = HEADS_PER_STEP
    qall = jnp.dot(qn, wq_ref[...], preferred_element_type=F32)
    kv = jnp.dot(kvn, wkv_ref[...].astype(BF16),
                 preferred_element_type=F32)
    kvw = NOPE + V_DIM
    for h in range(hb):
        vt_ref[0, h] = kv[:, h * kvw + NOPE:(h + 1) * kvw].T.astype(BF16)

    src = lax.broadcasted_iota(jnp.int32, (2 * LANES, LANES), 0)
    is_rope = src >= LANES
    src_even = ((src // HALF) % 2) == 0
    red_even = jnp.where(jnp.logical_and(is_rope, jnp.logical_not(src_even)), 0.0, 1.0).astype(BF16)
    red_odd = jnp.where(jnp.logical_and(is_rope, src_even), 0.0, 1.0).astype(BF16)
    red_kpe = jnp.where(is_rope, 0.5, 1.0).astype(BF16)

    def sumsq(nope, rope, red):
        sq = jnp.concatenate([(nope * nope).astype(BF16), (rope * rope).astype(BF16)], axis=1)
        return jnp.dot(sq, red, preferred_element_type=F32)

    c4 = c4_ref[...]
    s4 = s4_ref[...]
    kpe = kpe_ref[...].astype(F32)
    krope = _rope_pair(kpe * gk4_ref[...], c4, s4)

    lane = lax.broadcasted_iota(jnp.int32, (1, LANES), 1)
    even = ((lane // HALF) % 2) == 0
    gqn = gqn_ref[...] * q_scale
    gkn = gkn_ref[...]
    for p in range(hb // 2):
        base = hb * NOPE + p * LANES
        qr = qall[:, base:base + LANES]
        qrope = _rope_pair(qr * gq4_ref[...], c4, s4) * q_scale
        for e in range(2):
            h = 2 * p + e
            qh = qall[:, h * NOPE:(h + 1) * NOPE]
            rq = lax.rsqrt(sumsq(qh, qr, red_even if e == 0 else red_odd) * (1.0 / QK_DIM) + EPS)
            q_ref[0, h, :, 0:NOPE] = (qh * rq * gqn).astype(BF16)
            keep = even if e == 0 else jnp.logical_not(even)
            q_ref[0, h, :, NOPE:2 * NOPE] = (jnp.where(keep, qrope, 0.0) * rq).astype(BF16)
            kh = kv[:, h * kvw:h * kvw + NOPE]
            rk = lax.rsqrt(sumsq(kh, kpe, red_kpe) * (1.0 / QK_DIM) + EPS)
            k_ref[0, h, :, 0:NOPE] = (kh * rk * gkn).astype(BF16)
            k_ref[0, h, :, NOPE:2 * NOPE] = (krope * rk).astype(BF16)


def _qkv(za, gqa, gkva, wq, w_kv_up, l, gqn, gq4, gkn, gk4, c4, s4, B, S, q_scale, tm=512):
    T = za.shape[0]
    hb = HEADS_PER_STEP
    ng = N_HEADS // hb
    spt = S // tm
    qw = hb * QK_DIM
    vec = lambda n: pl.BlockSpec((1, n), lambda i, g: (0, 0))
    qk_spec = pl.BlockSpec((1, hb, tm, 2 * NOPE), lambda i, g: (i // spt, g, i % spt, 0))
    return pl.pallas_call(
        functools.partial(_qkv_kernel, q_scale=q_scale),
        grid=(T // tm, ng),
        in_specs=[
            pl.BlockSpec((tm, Q_RANK), lambda i, g: (i, 0)),
            pl.BlockSpec((tm, KV_RANK), lambda i, g: (i, Q_RANK // KV_RANK)),
            pl.BlockSpec((tm, LANES), lambda i, g: (i, LAT // LANES)),
            vec(Q_RANK), vec(KV_RANK),
            pl.BlockSpec((Q_RANK, qw), lambda i, g: (0, g)),
            pl.BlockSpec((None, KV_RANK, hb * (NOPE + V_DIM)), lambda i, g: (l, 0, g)),
            vec(LANES), vec(LANES), vec(LANES), vec(LANES),
            pl.BlockSpec((tm, LANES), lambda i, g: (i, 0)),
            pl.BlockSpec((tm, LANES), lambda i, g: (i, 0)),
        ],
        out_specs=[qk_spec, qk_spec,
                   pl.BlockSpec((1, hb, V_DIM, tm), lambda i, g: (i // spt, g, 0, i % spt))],
        out_shape=[jax.ShapeDtypeStruct((B, N_HEADS, S, 2 * NOPE), BF16),
                   jax.ShapeDtypeStruct((B, N_HEADS, S, 2 * NOPE), BF16),
                   jax.ShapeDtypeStruct((B, N_HEADS, V_DIM, S), BF16)],
        scratch_shapes=[pltpu.VMEM((tm, Q_RANK), BF16), pltpu.VMEM((tm, KV_RANK), BF16)],
        compiler_params=_cparams(("parallel", "arbitrary")),
        name="qkv",
    )(za, za, za, gqa, gkva, wq, w_kv_up, gqn, gq4, gkn, gk4, c4, s4)


def _attn_kernel(q_ref, k_ref, vt_ref, g_ref, o_ref, *, blk):
    S = q_ref.shape[2]
    nq = S // blk
    row = lax.broadcasted_iota(jnp.int32, (blk, blk), 0)
    colq = lax.broadcasted_iota(jnp.int32, (blk, blk), 1)
    causal = row <= colq
    nt = (((1,), (1,)), ((), ()))
    for h in range(HEADS_PER_STEP):
        for j in range(nq):
            lo = j * blk
            hi = lo + blk
            qj = q_ref[0, h, lo:hi, :]
            sd = lax.dot_general(k_ref[0, h, lo:hi, :], qj, nt, preferred_element_type=F32)
            sd = jnp.where(causal, sd, -jnp.inf)
            m = jnp.max(sd, axis=0, keepdims=True)
            if j > 0:
                sf = lax.dot_general(k_ref[0, h, 0:lo, :], qj, nt, preferred_element_type=F32)
                m = jnp.maximum(m, jnp.max(sf, axis=0, keepdims=True))
            pd = jnp.exp2(sd - m)
            l = jnp.sum(pd, axis=0, keepdims=True)
            acc = jnp.dot(vt_ref[0, h, :, lo:hi], pd.astype(BF16), preferred_element_type=F32)
            if j > 0:
                pf = jnp.exp2(sf - m)
                l = l + jnp.sum(pf, axis=0, keepdims=True)
                acc = acc + jnp.dot(vt_ref[0, h, :, 0:lo], pf.astype(BF16),
                                    preferred_element_type=F32)
            out = (acc * (1.0 / l)).T
            g = g_ref[lo:hi, h * V_DIM:(h + 1) * V_DIM].astype(F32)
            o_ref[lo:hi, h * V_DIM:(h + 1) * V_DIM] = (out * g).astype(BF16)


def _attn(q, k, vt, zg, blk=1024):
    B, H, S, E = q.shape
    hb = HEADS_PER_STEP
    ng = H // hb
    gw = hb * V_DIM
    qk_spec = pl.BlockSpec((1, hb, S, E), lambda b, g: (b, g, 0, 0))
    return pl.pallas_call(
        functools.partial(_attn_kernel, blk=blk),
        grid=(B, ng),
        in_specs=[qk_spec, qk_spec,
                  pl.BlockSpec((1, hb, V_DIM, S), lambda b, g: (b, g, 0, 0)),
                  pl.BlockSpec((S, gw), lambda b, g: (b, g))],
        out_specs=pl.BlockSpec((S, gw), lambda b, g: (b, g)),
        out_shape=jax.ShapeDtypeStruct((B * S, H * V_DIM), BF16),
        compiler_params=_cparams(("parallel", "parallel")),
        name="attn",
    )(q, k, vt, zg)


def _conv_kernel(u_ref, uh_ref, sg_ref, w_ref, bias_ref, lng_ref, lnb_ref,
                 o_ref, u_s, y_s, *, ts, tiles_per_seq):
    r = pl.program_id(0)
    C = u_ref.shape[1]
    first = (r % tiles_per_seq) == 0
    u_s[HALO:, :] = u_ref[...].astype(F32)
    u_s[0:HALO, :] = jnp.where(first, 0.0, uh_ref[...].astype(F32))

    rows = 128
    off = HALO - (CONV_K - 1)
    max_o = off + CONV_K - 1

    def lane_tile(c, carry):
        cs = pl.ds(pl.multiple_of(c * LANES, LANES), LANES)
        for r0 in range(0, ts, rows):
            acc = jnp.zeros((rows, LANES), F32) + bias_ref[:, cs]
            for s in range(SUBLANES):
                groups = [a for a in range(max_o // SUBLANES + 1)
                          if off <= SUBLANES * a + s <= max_o]
                n = rows if s == 0 else rows + SUBLANES
                z = None
                for a in groups:
                    k = SUBLANES * a + s - off
                    base = r0 + SUBLANES * a
                    t = u_s[base:base + n, cs] * w_ref[k:k + 1, cs]
                    z = t if z is None else z + t
                acc = acc + z[s:s + rows]
            y_s[r0:r0 + rows, cs] = acc
        return carry

    lax.fori_loop(0, C // LANES, lane_tile, 0)

    y = y_s[...]
    mu = jnp.mean(y, axis=-1, keepdims=True)
    yc = y - mu
    var = jnp.mean(yc * yc, axis=-1, keepdims=True)
    yn = yc * lax.rsqrt(var + EPS) * lng_ref[...] + lnb_ref[...]
    o_ref[...] = (_silu(yn) * sg_ref[...].astype(F32)).astype(BF16)


def _conv(u, zg, d_att, w_dw, b_dw, ln_g, ln_b, S, ts=256):
    T, C = u.shape
    hs = ts // HALO
    vec = pl.BlockSpec((1, C), lambda r: (0, 0))
    return pl.pallas_call(
        functools.partial(_conv_kernel, ts=ts, tiles_per_seq=S // ts),
        grid=(T // ts,),
        in_specs=[pl.BlockSpec((ts, C), lambda r: (r, 0)),
                  pl.BlockSpec((HALO, C), lambda r: (jnp.maximum(r * hs - 1, 0), 0)),
                  pl.BlockSpec((ts, C), lambda r: (r, d_att // C)),
                  pl.BlockSpec((CONV_K, C), lambda r: (0, 0)), vec, vec, vec],
        out_specs=pl.BlockSpec((ts, C), lambda r: (r, 0)),
        out_shape=jax.ShapeDtypeStruct((T, C), BF16),
        scratch_shapes=[pltpu.VMEM((ts + HALO, C), F32), pltpu.VMEM((ts, C), F32)],
        compiler_params=_cparams(("parallel",)),
        name="conv",
    )(u, u, zg, w_dw, b_dw, ln_g, ln_b)


def _out_kernel(att_ref, cv_ref, wa_ref, wc_ref, x_ref, o_ref, xb_ref, rinv_ref, ss_s, *, n_total):
    j = pl.program_id(1)
    y = jnp.dot(att_ref[...], wa_ref[...], preferred_element_type=F32)
    y = y + jnp.dot(cv_ref[...], wc_ref[...], preferred_element_type=F32)
    xn = x_ref[...] + y
    o_ref[...] = xn
    xb_ref[...] = xn.astype(BF16)
    part = jnp.sum(xn * xn, axis=-1, keepdims=True)

    @pl.when(j == 0)
    def _():
        ss_s[...] = part

    @pl.when(j > 0)
    def _():
        ss_s[...] = ss_s[...] + part

    @pl.when(j == pl.num_programs(1) - 1)
    def _():
        rinv = lax.rsqrt(ss_s[...] * (1.0 / n_total) + EPS)
        rinv_ref[...] = jnp.broadcast_to(rinv, rinv_ref.shape)


def _out_proj(att, cv, w_out, l, x, tm=1024, tn=512):
    T, A = att.shape
    Cw = cv.shape[1]
    N = w_out.shape[2]
    return pl.pallas_call(
        functools.partial(_out_kernel, n_total=N),
        grid=(T // tm, N // tn),
        in_specs=[pl.BlockSpec((tm, A), lambda i, j: (i, 0)),
                  pl.BlockSpec((tm, Cw), lambda i, j: (i, 0)),
                  pl.BlockSpec((None, A, tn), lambda i, j: (l, 0, j)),
                  pl.BlockSpec((None, Cw, tn), lambda i, j: (l, A // Cw, j)),
                  pl.BlockSpec((tm, tn), lambda i, j: (i, j))],
        out_specs=[pl.BlockSpec((tm, tn), lambda i, j: (i, j)),
                   pl.BlockSpec((tm, tn), lambda i, j: (i, j)),
                   pl.BlockSpec((tm, LANES), lambda i, j: (i, 0))],
        out_shape=[jax.ShapeDtypeStruct((T, N), F32), jax.ShapeDtypeStruct((T, N), BF16),
                   jax.ShapeDtypeStruct((T, LANES), F32)],
        scratch_shapes=[pltpu.VMEM((tm, 1), F32)],
        compiler_params=_cparams(("parallel", "arbitrary")),
        name="out_proj",
    )(att, cv, w_out, w_out, x)


def _pair_lanes(g):
    g1, g2 = g[..., :HALF], g[..., HALF:]
    return jnp.concatenate([g1, g1, g2, g2], axis=-1)


def kernel(x, positions, ln_g, w_in, q_a_norm, w_q_up, kv_a_norm, w_kv_up, q_norm, k_norm,
           w_dw, b_dw, conv_ln_g, conv_ln_b, w_out):
    B, S, D = x.shape
    L = w_in.shape[0]
    T = B * S
    d_att = N_HEADS * V_DIM
    d_conv = w_dw.shape[-1]
    d_rest = d_att + 3 * d_conv

    w_out_b = w_out.astype(BF16)
    w_in_t = jnp.swapaxes(w_in, 1, 2)

    inv_freq = ROPE_THETA ** (-jnp.arange(HALF, dtype=F32) / HALF)
    ang = positions.astype(F32).reshape(T, 1) * inv_freq
    cos, sin = jnp.cos(ang), jnp.sin(ang)
    c4 = jnp.concatenate([cos, cos, cos, cos], axis=-1)
    s4 = jnp.concatenate([-sin, -sin, sin, sin], axis=-1)
    q_scale = math.log2(math.e) / math.sqrt(QK_DIM)

    xf = x.reshape(T, D)
    xb, rinv = _norm0(xf)
    for l in range(L):
        w_lat, w_pe, w_rest = _prep_w_in(w_in_t, l, ln_g[l][None], d_rest)
        wq_l = _prep_w_q(w_q_up, l)
        za = _in_lat(xb, rinv, w_lat, w_pe)
        zg = _in_gate(xb, rinv, w_rest, d_att, d_conv)
        u = _in_glu(xb, rinv, w_rest, d_att, d_conv)
        q, k, vt = _qkv(za, q_a_norm[l][None], kv_a_norm[l][None], wq_l, w_kv_up, l,
                        q_norm[l][None, :NOPE], _pair_lanes(q_norm[l][None, NOPE:]),
                        k_norm[l][None, :NOPE], _pair_lanes(k_norm[l][None, NOPE:]),
                        c4, s4, B, S, q_scale)
        att = _attn(q, k, vt, zg)
        cv = _conv(u, zg, d_att, w_dw[l], b_dw[l][None], conv_ln_g[l][None], conv_ln_b[l][None], S)
        xf, xb, rinv = _out_proj(att, cv, w_out_b, l, xf)
    return xf.reshape(B, S, D)
```

```python
import functools
import math

import jax
import jax.numpy as jnp
from jax import lax
from jax.experimental import pallas as pl
from jax.experimental.pallas import tpu as pltpu

F32 = jnp.float32
BF16 = jnp.bfloat16

N_HEADS = 16
NOPE = 128
ROPE = 64
HALF = ROPE // 2
QK_DIM = NOPE + ROPE
V_DIM = 128
Q_RANK = 1024
KV_RANK = 512
LAT = Q_RANK + KV_RANK
CONV_K = 31
ROPE_THETA = 10000.0
EPS = 1e-6

LANES = 128
SUBLANES = 8
VMEM_LIMIT = 56 * 1024 * 1024

HEADS_PER_STEP = 4
HALO = 32


def _cparams(sem):
    return pltpu.CompilerParams(dimension_semantics=sem, vmem_limit_bytes=VMEM_LIMIT)


def _sigmoid(x):
    return 0.5 * jnp.tanh(0.5 * x) + 0.5


def _silu(x):
    h = 0.5 * x
    return h * jnp.tanh(h) + h


def _w_rows_kernel(w_ref, g_ref, o_ref):
    o_ref[...] = (w_ref[...] * g_ref[...]).astype(BF16)


def _w_pe_kernel(w_ref, g_ref, o_ref):
    x = (w_ref[...] * g_ref[...]).astype(BF16)
    for t, s in enumerate((0, 0, 1, 1)):
        o_ref[t * HALF:(t + 1) * HALF, :] = x[s * HALF:(s + 1) * HALF, :]


def _w_q_kernel(w_ref, o_ref):
    x = w_ref[...]
    hb = HEADS_PER_STEP
    parts = [x[:, h * QK_DIM:h * QK_DIM + NOPE] for h in range(hb)]

    def rp(h, half):
        s = h * QK_DIM + NOPE + half * HALF
        return x[:, s:s + HALF]

    for p in range(hb // 2):
        parts += [rp(2 * p, 0), rp(2 * p + 1, 0), rp(2 * p, 1), rp(2 * p + 1, 1)]
    o_ref[...] = jnp.concatenate(parts, axis=1).astype(BF16)


def _prep_w_in(w_in_t, l, g_row, d_rest, tl=512):
    K = w_in_t.shape[2]
    gspec = pl.BlockSpec((1, K), lambda i: (0, 0))

    def rows(name, n_rows, tr, first_block):
        return pl.pallas_call(
            _w_rows_kernel,
            grid=(pl.cdiv(n_rows, tr),),
            in_specs=[pl.BlockSpec((None, tr, K), lambda i: (l, first_block + i, 0)), gspec],
            out_specs=pl.BlockSpec((tr, K), lambda i: (i, 0)),
            out_shape=jax.ShapeDtypeStruct((n_rows, K), BF16),
            compiler_params=_cparams(("parallel",)),
            name=name,
        )(w_in_t, g_row)

    w_lat = rows("w_lat", LAT, tl, 0)
    tr_rest = (LAT + ROPE) // 4
    w_rest = rows("w_rest", d_rest, tr_rest, 4)
    w_pe = pl.pallas_call(
        _w_pe_kernel,
        grid=(1,),
        in_specs=[pl.BlockSpec((None, ROPE, K), lambda i: (l, LAT // ROPE, 0)), gspec],
        out_specs=pl.BlockSpec((4 * HALF, K), lambda i: (0, 0)),
        out_shape=jax.ShapeDtypeStruct((4 * HALF, K), BF16),
        compiler_params=_cparams(("arbitrary",)),
        name="w_pe",
    )(w_in_t, g_row)
    return w_lat, w_pe, w_rest


def _prep_w_q(w_q_up, l):
    _, R, N = w_q_up.shape
    gw = HEADS_PER_STEP * QK_DIM
    return pl.pallas_call(
        _w_q_kernel,
        grid=(N // gw,),
        in_specs=[pl.BlockSpec((None, R, gw), lambda g: (l, 0, g))],
        out_specs=pl.BlockSpec((R, gw), lambda g: (0, g)),
        out_shape=jax.ShapeDtypeStruct((R, N), BF16),
        compiler_params=_cparams(("parallel",)),
        name="w_q",
    )(w_q_up)


def _norm0_kernel(x_ref, xb_ref, rinv_ref):
    x = x_ref[...]
    xb_ref[...] = x.astype(BF16)
    rinv = lax.rsqrt(jnp.mean(x * x, axis=-1, keepdims=True) + EPS)
    rinv_ref[...] = jnp.broadcast_to(rinv, rinv_ref.shape)


def _norm0(x, tm=512):
    T, D = x.shape
    return pl.pallas_call(
        _norm0_kernel,
        grid=(T // tm,),
        in_specs=[pl.BlockSpec((tm, D), lambda i: (i, 0))],
        out_specs=[pl.BlockSpec((tm, D), lambda i: (i, 0)),
                   pl.BlockSpec((tm, LANES), lambda i: (i, 0))],
        out_shape=[jax.ShapeDtypeStruct((T, D), BF16), jax.ShapeDtypeStruct((T, LANES), F32)],
        compiler_params=_cparams(("parallel",)),
        name="norm0",
    )(x)


def _dot_nt(a, w_t):
    return lax.dot_general(a, w_t, (((1,), (1,)), ((), ())), preferred_element_type=F32)


def _lat_kernel(a_ref, r_ref, w_ref, wpe_ref, o_ref):
    a = a_ref[...]
    r = r_ref[:, 0:1]
    n = w_ref.shape[0]
    o_ref[:, 0:n] = (_dot_nt(a, w_ref[...]) * r).astype(BF16)
    o_ref[:, n:] = (_dot_nt(a, wpe_ref[...]) * r).astype(BF16)


def _gate_kernel(a_ref, r_ref, b_ref, o_ref):
    y = _dot_nt(a_ref[...], b_ref[...]) * r_ref[:, 0:1]
    o_ref[...] = _silu(y).astype(BF16)


def _glu_kernel(a_ref, r_ref, wa_ref, wb_ref, o_ref):
    r = r_ref[:, 0:1]
    ya = _dot_nt(a_ref[...], wa_ref[...]) * r
    yb = _dot_nt(a_ref[...], wb_ref[...]) * r
    o_ref[...] = (ya * _sigmoid(yb)).astype(BF16)


def _in_lat(xb, rinv, w_lat, w_pe, tm=512):
    T, K = xb.shape
    N = w_lat.shape[0] + w_pe.shape[0]
    return pl.pallas_call(
        _lat_kernel,
        grid=(T // tm,),
        in_specs=[pl.BlockSpec((tm, K), lambda i: (i, 0)),
                  pl.BlockSpec((tm, LANES), lambda i: (i, 0)),
                  pl.BlockSpec(w_lat.shape, lambda i: (0, 0)),
                  pl.BlockSpec(w_pe.shape, lambda i: (0, 0))],
        out_specs=pl.BlockSpec((tm, N), lambda i: (i, 0)),
        out_shape=jax.ShapeDtypeStruct((T, N), BF16),
        compiler_params=_cparams(("parallel",)),
        name="in_lat",
    )(xb, rinv, w_lat, w_pe)


def _in_gate(xb, rinv, w_rest, d_att, d_conv, tm=1024, tn=1024):
    T, K = xb.shape
    n_att = d_att // tn
    skip = (2 * d_conv) // tn
    return pl.pallas_call(
        _gate_kernel,
        grid=(T // tm, (d_att + d_conv) // tn),
        in_specs=[pl.BlockSpec((tm, K), lambda i, j: (i, 0)),
                  pl.BlockSpec((tm, LANES), lambda i, j: (i, 0)),
                  pl.BlockSpec((tn, K), lambda i, j: (jnp.where(j < n_att, j, j + skip), 0))],
        out_specs=pl.BlockSpec((tm, tn), lambda i, j: (i, j)),
        out_shape=jax.ShapeDtypeStruct((T, d_att + d_conv), BF16),
        compiler_params=_cparams(("parallel", "arbitrary")),
        name="in_gate",
    )(xb, rinv, w_rest)


def _in_glu(xb, rinv, w_rest, d_att, d_conv, tm=1024, tn=512):
    T, K = xb.shape
    ca = d_att // tn
    cb = (d_att + d_conv) // tn
    return pl.pallas_call(
        _glu_kernel,
        grid=(T // tm, d_conv // tn),
        in_specs=[pl.BlockSpec((tm, K), lambda i, j: (i, 0)),
                  pl.BlockSpec((tm, LANES), lambda i, j: (i, 0)),
                  pl.BlockSpec((tn, K), lambda i, j: (ca + j, 0)),
                  pl.BlockSpec((tn, K), lambda i, j: (cb + j, 0))],
        out_specs=pl.BlockSpec((tm, tn), lambda i, j: (i, j)),
        out_shape=jax.ShapeDtypeStruct((T, d_conv), BF16),
        compiler_params=_cparams(("parallel", "arbitrary")),
        name="in_glu",
    )(xb, rinv, w_rest, w_rest)


def _rope_pair(x, c4, s4):
    return x * c4 + pltpu.roll(x, 2 * HALF, axis=1) * s4


def _qkv_kernel(qc_ref, kvc_ref, kpe_ref, gqa_ref, gkva_ref, wq_ref, wkv_ref,
                gqn_ref, gq4_ref, gkn_ref, gk4_ref, c4_ref, s4_ref,
                q_ref, k_ref, vt_ref, qn_s, kvn_s, *, q_scale):
    hg = pl.program_id(1)

    @pl.when(hg == 0)
    def _():
        qc = qc_ref[...].astype(F32)
        r = lax.rsqrt(jnp.mean(qc * qc, axis=-1, keepdims=True) + EPS)
        qn_s[...] = (qc * r * gqa_ref[...]).astype(BF16)
        kvc = kvc_ref[...].astype(F32)
        r = lax.rsqrt(jnp.mean(kvc * kvc, axis=-1, keepdims=True) + EPS)
        kvn_s[...] = (kvc * r * gkva_ref[...]).astype(BF16)

    qn = qn_s[...]
    kvn = kvn_s[...]
    hb = HEADS_PER_STEP
    qall = jnp.dot(qn, wq_ref[...], preferred_element_type=F32)
    kv = jnp.dot(kvn, wkv_ref[...].astype(BF16),
                 preferred_element_type=F32)
    kvw = NOPE + V_DIM
    for h in range(hb):
        vt_ref[0, h] = kv[:, h * kvw + NOPE:(h + 1) * kvw].T.astype(BF16)

    src = lax.broadcasted_iota(jnp.int32, (2 * LANES, LANES), 0)
    is_rope = src >= LANES
    src_even = ((src // HALF) % 2) == 0
    red_even = jnp.where(jnp.logical_and(is_rope, jnp.logical_not(src_even)), 0.0, 1.0).astype(BF16)
    red_odd = jnp.where(jnp.logical_and(is_rope, src_even), 0.0, 1.0).astype(BF16)
    red_kpe = jnp.where(is_rope, 0.5, 1.0).astype(BF16)

    def sumsq(nope, rope, red):
        sq = jnp.concatenate([(nope * nope).astype(BF16), (rope * rope).astype(BF16)], axis=1)
        return jnp.dot(sq, red, preferred_element_type=F32)

    c4 = c4_ref[...]
    s4 = s4_ref[...]
    kpe = kpe_ref[...].astype(F32)
    krope = _rope_pair(kpe * gk4_ref[...], c4, s4)

    lane = lax.broadcasted_iota(jnp.int32, (1, LANES), 1)
    even = ((lane // HALF) % 2) == 0
    gqn = gqn_ref[...] * q_scale
    gkn = gkn_ref[...]
    for p in range(hb // 2):
        base = hb * NOPE + p * LANES
        qr = qall[:, base:base + LANES]
        qrope = _rope_pair(qr * gq4_ref[...], c4, s4) * q_scale
        for e in range(2):
            h = 2 * p + e
            qh = qall[:, h * NOPE:(h + 1) * NOPE]
            rq = lax.rsqrt(sumsq(qh, qr, red_even if e == 0 else red_odd) * (1.0 / QK_DIM) + EPS)
            q_ref[0, h, :, 0:NOPE] = (qh * rq * gqn).astype(BF16)
            keep = even if e == 0 else jnp.logical_not(even)
            q_ref[0, h, :, NOPE:2 * NOPE] = (jnp.where(keep, qrope, 0.0) * rq).astype(BF16)
            kh = kv[:, h * kvw:h * kvw + NOPE]
            rk = lax.rsqrt(sumsq(kh, kpe, red_kpe) * (1.0 / QK_DIM) + EPS)
            k_ref[0, h, :, 0:NOPE] = (kh * rk * gkn).astype(BF16)
            k_ref[0, h, :, NOPE:2 * NOPE] = (krope * rk).astype(BF16)


def _qkv(za, gqa, gkva, wq, w_kv_up, l, gqn, gq4, gkn, gk4, c4, s4, B, S, q_scale, tm=512):
    T = za.shape[0]
    hb = HEADS_PER_STEP
    ng = N_HEADS // hb
    spt = S // tm
    qw = hb * QK_DIM
    vec = lambda n: pl.BlockSpec((1, n), lambda i, g: (0, 0))
    qk_spec = pl.BlockSpec((1, hb, tm, 2 * NOPE), lambda i, g: (i // spt, g, i % spt, 0))
    return pl.pallas_call(
        functools.partial(_qkv_kernel, q_scale=q_scale),
        grid=(T // tm, ng),
        in_specs=[
            pl.BlockSpec((tm, Q_RANK), lambda i, g: (i, 0)),
            pl.BlockSpec((tm, KV_RANK), lambda i, g: (i, Q_RANK // KV_RANK)),
            pl.BlockSpec((tm, LANES), lambda i, g: (i, LAT // LANES)),
            vec(Q_RANK), vec(KV_RANK),
            pl.BlockSpec((Q_RANK, qw), lambda i, g: (0, g)),
            pl.BlockSpec((None, KV_RANK, hb * (NOPE + V_DIM)), lambda i, g: (l, 0, g)),
            vec(LANES), vec(LANES), vec(LANES), vec(LANES),
            pl.BlockSpec((tm, LANES), lambda i, g: (i, 0)),
            pl.BlockSpec((tm, LANES), lambda i, g: (i, 0)),
        ],
        out_specs=[qk_spec, qk_spec,
                   pl.BlockSpec((1, hb, V_DIM, tm), lambda i, g: (i // spt, g, 0, i % spt))],
        out_shape=[jax.ShapeDtypeStruct((B, N_HEADS, S, 2 * NOPE), BF16),
                   jax.ShapeDtypeStruct((B, N_HEADS, S, 2 * NOPE), BF16),
                   jax.ShapeDtypeStruct((B, N_HEADS, V_DIM, S), BF16)],
        scratch_shapes=[pltpu.VMEM((tm, Q_RANK), BF16), pltpu.VMEM((tm, KV_RANK), BF16)],
        compiler_params=_cparams(("parallel", "arbitrary")),
        name="qkv",
    )(za, za, za, gqa, gkva, wq, w_kv_up, gqn, gq4, gkn, gk4, c4, s4)


def _attn_kernel(q_ref, k_ref, vt_ref, g_ref, o_ref, *, blk):
    S = q_ref.shape[2]
    nq = S // blk
    row = lax.broadcasted_iota(jnp.int32, (blk, blk), 0)
    colq = lax.broadcasted_iota(jnp.int32, (blk, blk), 1)
    causal = row <= colq
    nt = (((1,), (1,)), ((), ()))
    for h in range(HEADS_PER_STEP):
        for j in range(nq):
            lo = j * blk
            hi = lo + blk
            qj = q_ref[0, h, lo:hi, :]
            sd = lax.dot_general(k_ref[0, h, lo:hi, :], qj, nt, preferred_element_type=F32)
            sd = jnp.where(causal, sd, -jnp.inf)
            m = jnp.max(sd, axis=0, keepdims=True)
            if j > 0:
                sf = lax.dot_general(k_ref[0, h, 0:lo, :], qj, nt, preferred_element_type=F32)
                m = jnp.maximum(m, jnp.max(sf, axis=0, keepdims=True))
            pd = jnp.exp2(sd - m)
            l = jnp.sum(pd, axis=0, keepdims=True)
            acc = jnp.dot(vt_ref[0, h, :, lo:hi], pd.astype(BF16), preferred_element_type=F32)
            if j > 0:
                pf = jnp.exp2(sf - m)
                l = l + jnp.sum(pf, axis=0, keepdims=True)
                acc = acc + jnp.dot(vt_ref[0, h, :, 0:lo], pf.astype(BF16),
                                    preferred_element_type=F32)
            out = (acc * (1.0 / l)).T
            g = g_ref[lo:hi, h * V_DIM:(h + 1) * V_DIM].astype(F32)
            o_ref[lo:hi, h * V_DIM:(h + 1) * V_DIM] = (out * g).astype(BF16)


def _attn(q, k, vt, zg, blk=1024):
    B, H, S, E = q.shape
    hb = HEADS_PER_STEP
    ng = H // hb
    gw = hb * V_DIM
    qk_spec = pl.BlockSpec((1, hb, S, E), lambda b, g: (b, g, 0, 0))
    return pl.pallas_call(
        functools.partial(_attn_kernel, blk=blk),
        grid=(B, ng),
        in_specs=[qk_spec, qk_spec,
                  pl.BlockSpec((1, hb, V_DIM, S), lambda b, g: (b, g, 0, 0)),
                  pl.BlockSpec((S, gw), lambda b, g: (b, g))],
        out_specs=pl.BlockSpec((S, gw), lambda b, g: (b, g)),
        out_shape=jax.ShapeDtypeStruct((B * S, H * V_DIM), BF16),
        compiler_params=_cparams(("parallel", "parallel")),
        name="attn",
    )(q, k, vt, zg)


def _conv_kernel(u_ref, uh_ref, sg_ref, w_ref, bias_ref, lng_ref, lnb_ref,
                 o_ref, u_s, y_s, *, ts, tiles_per_seq):
    r = pl.program_id(0)
    C = u_ref.shape[1]
    first = (r % tiles_per_seq) == 0
    u_s[HALO:, :] = u_ref[...].astype(F32)
    u_s[0:HALO, :] = jnp.where(first, 0.0, uh_ref[...].astype(F32))

    rows = 128
    off = HALO - (CONV_K - 1)
    max_o = off + CONV_K - 1

    def lane_tile(c, carry):
        cs = pl.ds(pl.multiple_of(c * LANES, LANES), LANES)
        for r0 in range(0, ts, rows):
            acc = jnp.zeros((rows, LANES), F32) + bias_ref[:, cs]
            for s in range(SUBLANES):
                groups = [a for a in range(max_o // SUBLANES + 1)
                          if off <= SUBLANES * a + s <= max_o]
                n = rows if s == 0 else rows + SUBLANES
                z = None
                for a in groups:
                    k = SUBLANES * a + s - off
                    base = r0 + SUBLANES * a
                    t = u_s[base:base + n, cs] * w_ref[k:k + 1, cs]
                    z = t if z is None else z + t
                acc = acc + z[s:s + rows]
            y_s[r0:r0 + rows, cs] = acc
        return carry

    lax.fori_loop(0, C // LANES, lane_tile, 0)

    y = y_s[...]
    mu = jnp.mean(y, axis=-1, keepdims=True)
    yc = y - mu
    var = jnp.mean(yc * yc, axis=-1, keepdims=True)
    yn = yc * lax.rsqrt(var + EPS) * lng_ref[...] + lnb_ref[...]
    o_ref[...] = (_silu(yn) * sg_ref[...].astype(F32)).astype(BF16)


def _conv(u, zg, d_att, w_dw, b_dw, ln_g, ln_b, S, ts=256):
    T, C = u.shape
    hs = ts // HALO
    vec = pl.BlockSpec((1, C), lambda r: (0, 0))
    return pl.pallas_call(
        functools.partial(_conv_kernel, ts=ts, tiles_per_seq=S // ts),
        grid=(T // ts,),
        in_specs=[pl.BlockSpec((ts, C), lambda r: (r, 0)),
                  pl.BlockSpec((HALO, C), lambda r: (jnp.maximum(r * hs - 1, 0), 0)),
                  pl.BlockSpec((ts, C), lambda r: (r, d_att // C)),
                  pl.BlockSpec((CONV_K, C), lambda r: (0, 0)), vec, vec, vec],
        out_specs=pl.BlockSpec((ts, C), lambda r: (r, 0)),
        out_shape=jax.ShapeDtypeStruct((T, C), BF16),
        scratch_shapes=[pltpu.VMEM((ts + HALO, C), F32), pltpu.VMEM((ts, C), F32)],
        compiler_params=_cparams(("parallel",)),
        name="conv",
    )(u, u, zg, w_dw, b_dw, ln_g, ln_b)


def _out_kernel(att_ref, cv_ref, wa_ref, wc_ref, x_ref, o_ref, xb_ref, rinv_ref, ss_s, *, n_total):
    j = pl.program_id(1)
    y = jnp.dot(att_ref[...], wa_ref[...], preferred_element_type=F32)
    y = y + jnp.dot(cv_ref[...], wc_ref[...], preferred_element_type=F32)
    xn = x_ref[...] + y
    o_ref[...] = xn
    xb_ref[...] = xn.astype(BF16)
    part = jnp.sum(xn * xn, axis=-1, keepdims=True)

    @pl.when(j == 0)
    def _():
        ss_s[...] = part

    @pl.when(j > 0)
    def _():
        ss_s[...] = ss_s[...] + part

    @pl.when(j == pl.num_programs(1) - 1)
    def _():
        rinv = lax.rsqrt(ss_s[...] * (1.0 / n_total) + EPS)
        rinv_ref[...] = jnp.broadcast_to(rinv, rinv_ref.shape)


def _out_proj(att, cv, w_out, l, x, tm=1024, tn=512):
    T, A = att.shape
    Cw = cv.shape[1]
    N = w_out.shape[2]
    return pl.pallas_call(
        functools.partial(_out_kernel, n_total=N),
        grid=(T // tm, N // tn),
        in_specs=[pl.BlockSpec((tm, A), lambda i, j: (i, 0)),
                  pl.BlockSpec((tm, Cw), lambda i, j: (i, 0)),
                  pl.BlockSpec((None, A, tn), lambda i, j: (l, 0, j)),
                  pl.BlockSpec((None, Cw, tn), lambda i, j: (l, A // Cw, j)),
                  pl.BlockSpec((tm, tn), lambda i, j: (i, j))],
        out_specs=[pl.BlockSpec((tm, tn), lambda i, j: (i, j)),
                   pl.BlockSpec((tm, tn), lambda i, j: (i, j)),
                   pl.BlockSpec((tm, LANES), lambda i, j: (i, 0))],
        out_shape=[jax.ShapeDtypeStruct((T, N), F32), jax.ShapeDtypeStruct((T, N), BF16),
                   jax.ShapeDtypeStruct((T, LANES), F32)],
        scratch_shapes=[pltpu.VMEM((tm, 1), F32)],
        compiler_params=_cparams(("parallel", "arbitrary")),
        name="out_proj",
    )(att, cv, w_out, w_out, x)


def _pair_lanes(g):
    g1, g2 = g[..., :HALF], g[..., HALF:]
    return jnp.concatenate([g1, g1, g2, g2], axis=-1)


def kernel(x, positions, ln_g, w_in, q_a_norm, w_q_up, kv_a_norm, w_kv_up, q_norm, k_norm,
           w_dw, b_dw, conv_ln_g, conv_ln_b, w_out):
    B, S, D = x.shape
    L = w_in.shape[0]
    T = B * S
    d_att = N_HEADS * V_DIM
    d_conv = w_dw.shape[-1]
    d_rest = d_att + 3 * d_conv

    w_out_b = w_out.astype(BF16)
    w_in_t = jnp.swapaxes(w_in, 1, 2)

    inv_freq = ROPE_THETA ** (-jnp.arange(HALF, dtype=F32) / HALF)
    ang = positions.astype(F32).reshape(T, 1) * inv_freq
    cos, sin = jnp.cos(ang), jnp.sin(ang)
    c4 = jnp.concatenate([cos, cos, cos, cos], axis=-1)
    s4 = jnp.concatenate([-sin, -sin, sin, sin], axis=-1)
    q_scale = math.log2(math.e) / math.sqrt(QK_DIM)

    xf = x.reshape(T, D)
    xb, rinv = _norm0(xf)
    for l in range(L):
        w_lat, w_pe, w_rest = _prep_w_in(w_in_t, l, ln_g[l][None], d_rest)
        wq_l = _prep_w_q(w_q_up, l)
        za = _in_lat(xb, rinv, w_lat, w_pe)
        zg = _in_gate(xb, rinv, w_rest, d_att, d_conv)
        u = _in_glu(xb, rinv, w_rest, d_att, d_conv)
        q, k, vt = _qkv(za, q_a_norm[l][None], kv_a_norm[l][None], wq_l, w_kv_up, l,
                        q_norm[l][None, :NOPE], _pair_lanes(q_norm[l][None, NOPE:]),
                        k_norm[l][None, :NOPE], _pair_lanes(k_norm[l][None, NOPE:]),
                        c4, s4, B, S, q_scale)
        att = _attn(q, k, vt, zg)
        cv = _conv(u, zg, d_att, w_dw[l], b_dw[l][None], conv_ln_g[l][None], conv_ln_b[l][None], S)
        xf, xb, rinv = _out_proj(att, cv, w_out_b, l, xf)
    return xf.reshape(B, S, D)
```
